```python
import math
import jax, jax.numpy as jnp
from jax import lax
import numpy as np

D_MODEL = 2048
BATCH = 4
SEQ = 4096
DEPTH = 4

N_MIXERS = 2
N_A_LAYERS = (DEPTH + 1) // 2
N_B_LAYERS = DEPTH // 2

CHUNK = 128
GMLP_FFN = 6 * D_MODEL
GMLP_HALF = GMLP_FFN // 2
GMLP_GROUPS = 8
GMLP_GROUP_W = GMLP_HALF // GMLP_GROUPS

HEAD_DIM = 64
N_Q_HEADS = D_MODEL // HEAD_DIM
N_KV_HEADS = max(N_Q_HEADS // 8, 1)
Q_PER_KV = N_Q_HEADS // N_KV_HEADS
QKV_W = (N_Q_HEADS + 2 * N_KV_HEADS) * HEAD_DIM
WINDOW = 128
ATT_BLOCK = WINDOW

N_EXPERTS = 64
TOP_K = 8
EXPERT_FF = D_MODEL // 8
N_EXPERT_GROUPS = 8
TOPK_GROUPS = 4
ROUTED_SCALE = 2.5
MOE_BLOCK = 128

DEEPNORM_ALPHA = (2.0 * DEPTH) ** 0.25
DEEPNORM_BETA = (8.0 * DEPTH) ** -0.25
LN_EPS = 1e-5
NEG_INF = -1e30

kernel_name = 'hybrid_gmlp_swa_sink_moe_deepnorm_adaln'


def layer_norm(x, g, b):
    xf = x.astype(jnp.float32)
    mu = jnp.mean(xf, axis=-1, keepdims=True)
    var = jnp.mean(jnp.square(xf - mu), axis=-1, keepdims=True)
    y = (xf - mu) * lax.rsqrt(var + LN_EPS) * g.astype(jnp.float32) + b.astype(jnp.float32)
    return y.astype(x.dtype)


def ada_mod(c_act, w, b):
    m = c_act @ w + b
    shift, scale, gate = jnp.split(m[:, None, :], 3, axis=-1)
    return shift, scale, gate


def alibi_slopes(n):
    p = 2 ** int(math.floor(math.log2(n)))
    base = [2.0 ** (-8.0 * (i + 1) / p) for i in range(p)]
    extra = [2.0 ** (-4.0 * (2 * i + 1) / p) for i in range(n - p)]
    return np.array(base + extra, dtype=np.float32)


def gmlp_chunk_mixer(h, w_in, b_in, v_ln_g, v_ln_b, w_s, b_s, w_out, b_out):
    B, S, _ = h.shape
    nc = S // CHUNK
    z = jax.nn.gelu(h @ w_in + b_in, approximate=False)
    u, v = jnp.split(z, 2, axis=-1)
    v = layer_norm(v, v_ln_g, v_ln_b)
    v = v.reshape(B, nc, CHUNK, GMLP_GROUPS, GMLP_GROUP_W)
    causal = jnp.tril(jnp.ones((CHUNK, CHUNK), dtype=bool))
    ws = jnp.where(causal[None], w_s, 0)
    s = jnp.einsum('gts,bcsgw->bctgw', ws, v) + jnp.transpose(b_s)[None, None, :, :, None]
    s = s.reshape(B, S, GMLP_HALF)
    return (u * s) @ w_out + b_out


def swa_sink_attention(h, w_qkv, b_qkv, sinks, w_o, b_o):
    B, S, _ = h.shape
    nb = S // ATT_BLOCK
    qkv = h @ w_qkv + b_qkv
    q, k, v = jnp.split(qkv, [N_Q_HEADS * HEAD_DIM, (N_Q_HEADS + N_KV_HEADS) * HEAD_DIM], axis=-1)
    q = q.reshape(B, nb, ATT_BLOCK, N_KV_HEADS, Q_PER_KV, HEAD_DIM)
    k = k.reshape(B, nb, ATT_BLOCK, N_KV_HEADS, HEAD_DIM)
    v = v.reshape(B, nb, ATT_BLOCK, N_KV_HEADS, HEAD_DIM)

    def band(t):
        prev = jnp.pad(t[:, :-1], ((0, 0), (1, 0), (0, 0), (0, 0), (0, 0)))
        return jnp.concatenate([prev, t], axis=2)

    kb, vb = band(k), band(v)
    scores = jnp.einsum('bnqhgd,bnkhd->bnhgqk', q, kb,
                        preferred_element_type=jnp.float32) * (HEAD_DIM ** -0.5)
    qi = jnp.arange(ATT_BLOCK)[:, None]
    kj = jnp.arange(2 * ATT_BLOCK)[None, :]
    dist = qi + ATT_BLOCK - kj
    blk = jnp.arange(nb)[:, None, None]
    valid = (dist >= 0) & (dist < WINDOW) & (blk * ATT_BLOCK - ATT_BLOCK + kj >= 0)
    slopes = jnp.asarray(alibi_slopes(N_Q_HEADS)).reshape(N_KV_HEADS, Q_PER_KV)
    scores = scores - slopes[:, :, None, None] * dist.astype(jnp.float32)
    scores = jnp.where(valid[None, :, None, None], scores, NEG_INF)
    sink = sinks.astype(jnp.float32).reshape(N_KV_HEADS, Q_PER_KV)[None, None, :, :, None, None]
    m = jnp.maximum(jnp.max(scores, axis=-1, keepdims=True), sink)
    p = jnp.exp(scores - m)
    p = p / (jnp.sum(p, axis=-1, keepdims=True) + jnp.exp(sink - m))
    out = jnp.einsum('bnhgqk,bnkhd->bnqhgd', p.astype(vb.dtype), vb)
    out = out.reshape(B, S, N_Q_HEADS * HEAD_DIM)
    return out @ w_o + b_o


def moe_ffn(h, w_router, router_bias, w_gate, w_up, w_down, ws_gate, ws_up, ws_down):
    B, S, D = h.shape
    T = B * S
    xf = h.reshape(T, D)
    scores = jax.nn.sigmoid(jnp.dot(xf, w_router, preferred_element_type=jnp.float32))
    sel = scores + router_bias.astype(jnp.float32)
    grp_score = jnp.sum(lax.top_k(sel.reshape(T, N_EXPERT_GROUPS, -1), 2)[0], axis=-1)
    _, top_groups = lax.top_k(grp_score, TOPK_GROUPS)
    gmask = jnp.any(top_groups[:, :, None] == jnp.arange(N_EXPERT_GROUPS)[None, None, :], axis=1)
    sel = jnp.where(jnp.repeat(gmask, N_EXPERTS // N_EXPERT_GROUPS, axis=1), sel, NEG_INF)
    _, eidx = lax.top_k(sel, TOP_K)
    w = jnp.take_along_axis(scores, eidx, axis=1)
    w = w / jnp.sum(w, axis=-1, keepdims=True) * ROUTED_SCALE

    TK = T * TOP_K
    flat_e = eidx.reshape(TK)
    flat_t = jnp.repeat(jnp.arange(T, dtype=jnp.int32), TOP_K)
    flat_w = w.reshape(TK)
    order = jnp.argsort(flat_e)
    se = flat_e[order]
    counts = jnp.zeros((N_EXPERTS,), jnp.int32).at[flat_e].add(1)
    padded = (counts + MOE_BLOCK - 1) // MOE_BLOCK * MOE_BLOCK
    starts = jnp.cumsum(counts) - counts
    pends = jnp.cumsum(padded)
    pstarts = pends - padded
    dest = pstarts[se] + jnp.arange(TK, dtype=jnp.int32) - starts[se]
    nblk = -(-TK // MOE_BLOCK) + N_EXPERTS
    P = nblk * MOE_BLOCK
    row_tok = jnp.full((P,), T, jnp.int32).at[dest].set(flat_t[order])
    row_w = jnp.zeros((P,), jnp.float32).at[dest].set(flat_w[order])
    blk_start = jnp.arange(nblk, dtype=jnp.int32) * MOE_BLOCK
    blk_e = jnp.minimum(jnp.sum(blk_start[:, None] >= pends[None, :], axis=1), N_EXPERTS - 1)
    x_pad = jnp.concatenate([xf, jnp.zeros((1, D), xf.dtype)], axis=0)

    def expert_block(args):
        tok, e = args
        xb = x_pad[tok]
        return (jax.nn.silu(xb @ w_gate[e]) * (xb @ w_up[e])) @ w_down[e]

    ys = lax.map(expert_block, (row_tok.reshape(nblk, MOE_BLOCK), blk_e)).reshape(P, D)
    ys = ys * row_w[:, None].astype(ys.dtype)
    routed = jax.ops.segment_sum(ys, row_tok, num_segments=T + 1)[:T]
    shared = (jax.nn.silu(xf @ ws_gate) * (xf @ ws_up)) @ ws_down
    return (routed + shared).reshape(B, S, D)


def setup_inputs(seed: int = 0) -> dict:
    key = jax.random.key(seed)
    ks = iter(jax.random.split(key, 32))

    def nrm(shape, std):
        return jax.random.normal(next(ks), shape, jnp.float32) * std

    D = D_MODEL
    return {
        'x': nrm((BATCH, SEQ, D), 1.0),
        'c': nrm((BATCH, D), 1.0),
        'ada_w': nrm((DEPTH, 2, D, 3 * D), 0.1 * D ** -0.5),
        'ada_b': nrm((DEPTH, 2, 3 * D), 0.01),
        'ln_g': 1.0 + nrm((DEPTH, 2, D), 0.01),
        'ln_b': nrm((DEPTH, 2, D), 0.01),
        'gm_w_in': nrm((N_A_LAYERS, D, 2 * GMLP_HALF), D ** -0.5),
        'gm_b_in': nrm((N_A_LAYERS, 2 * GMLP_HALF), 0.01),
        'gm_v_ln_g': 1.0 + nrm((N_A_LAYERS, GMLP_HALF), 0.01),
        'gm_v_ln_b': nrm((N_A_LAYERS, GMLP_HALF), 0.01),
        'gm_w_s': nrm((N_A_LAYERS, GMLP_GROUPS, CHUNK, CHUNK), CHUNK ** -0.5),
        'gm_b_s': 1.0 + nrm((N_A_LAYERS, GMLP_GROUPS, CHUNK), 0.01),
        'gm_w_out': nrm((N_A_LAYERS, GMLP_HALF, D), GMLP_HALF ** -0.5 * DEEPNORM_BETA),
        'gm_b_out': nrm((N_A_LAYERS, D), 0.01),
        'at_w_qkv': nrm((N_B_LAYERS, D, QKV_W), D ** -0.5),
        'at_b_qkv': nrm((N_B_LAYERS, QKV_W), 0.01),
        'at_sinks': nrm((N_B_LAYERS, N_Q_HEADS), 1.0),
        'at_w_o': nrm((N_B_LAYERS, N_Q_HEADS * HEAD_DIM, D), (N_Q_HEADS * HEAD_DIM) ** -0.5 * DEEPNORM_BETA),
        'at_b_o': nrm((N_B_LAYERS, D), 0.01),
        'moe_w_router': nrm((DEPTH, D, N_EXPERTS), D ** -0.5),
        'moe_router_bias': nrm((DEPTH, N_EXPERTS), 0.01),
        'moe_w_gate': nrm((DEPTH, N_EXPERTS, D, EXPERT_FF), D ** -0.5),
        'moe_w_up': nrm((DEPTH, N_EXPERTS, D, EXPERT_FF), D ** -0.5),
        'moe_w_down': nrm((DEPTH, N_EXPERTS, EXPERT_FF, D), EXPERT_FF ** -0.5 * DEEPNORM_BETA),
        'moe_ws_gate': nrm((DEPTH, D, EXPERT_FF), D ** -0.5),
        'moe_ws_up': nrm((DEPTH, D, EXPERT_FF), D ** -0.5),
        'moe_ws_down': nrm((DEPTH, EXPERT_FF, D), EXPERT_FF ** -0.5 * DEEPNORM_BETA),
    }


def reference(x, c, ada_w, ada_b, ln_g, ln_b,
              gm_w_in, gm_b_in, gm_v_ln_g, gm_v_ln_b, gm_w_s, gm_b_s, gm_w_out, gm_b_out,
              at_w_qkv, at_b_qkv, at_sinks, at_w_o, at_b_o,
              moe_w_router, moe_router_bias, moe_w_gate, moe_w_up, moe_w_down,
              moe_ws_gate, moe_ws_up, moe_ws_down):
    c_act = jax.nn.silu(c)
    for i in range(DEPTH):
        j = i // N_MIXERS
        shift, scale, gate = ada_mod(c_act, ada_w[i, 0], ada_b[i, 0])
        h = x * (1 + scale) + shift
        if i % N_MIXERS == 0:
            y = gmlp_chunk_mixer(h, gm_w_in[j], gm_b_in[j], gm_v_ln_g[j], gm_v_ln_b[j],
                                 gm_w_s[j], gm_b_s[j], gm_w_out[j], gm_b_out[j])
        else:
            y = swa_sink_attention(h, at_w_qkv[j], at_b_qkv[j], at_sinks[j], at_w_o[j], at_b_o[j])
        x = layer_norm(DEEPNORM_ALPHA * x + (1 + gate) * y, ln_g[i, 0], ln_b[i, 0])
        shift, scale, gate = ada_mod(c_act, ada_w[i, 1], ada_b[i, 1])
        h = x * (1 + scale) + shift
        y = moe_ffn(h, moe_w_router[i], moe_router_bias[i], moe_w_gate[i], moe_w_up[i],
                    moe_w_down[i], moe_ws_gate[i], moe_ws_up[i], moe_ws_down[i])
        x = layer_norm(DEEPNORM_ALPHA * x + (1 + gate) * y, ln_g[i, 1], ln_b[i, 1])
    return x
```

```python
import functools
import math

import numpy as np
import jax
import jax.numpy as jnp
from jax import lax
from jax.experimental import pallas as pl
from jax.experimental.pallas import tpu as pltpu

F32 = jnp.float32
BF16 = jnp.bfloat16

DEPTH = 4
CHUNK = 128
GMLP_GROUPS = 8
HEAD_DIM = 64
Q_PER_KV = 8
ATT_BLOCK = 128
N_EXPERTS = 64
TOP_K = 8
N_EXPERT_GROUPS = 8
TOPK_GROUPS = 4
ROUTED_SCALE = 2.5
DEEPNORM_ALPHA = (2.0 * DEPTH) ** 0.25
LN_EPS = 1e-5
NEG_INF = -1e30

LANES = 128
VMEM_LIMIT = 56 * 1024 * 1024

MOE_BM = 256
COMB_TT = 128


def _cp(sem, vmem=VMEM_LIMIT):
    return pltpu.CompilerParams(dimension_semantics=sem, vmem_limit_bytes=vmem)


def _layer_norm(x, g, b):
    mu = jnp.mean(x, axis=-1, keepdims=True)
    xc = x - mu
    var = jnp.mean(xc * xc, axis=-1, keepdims=True)
    return xc * lax.rsqrt(var + LN_EPS) * g + b


def _alibi_slopes(n):
    p = 2 ** int(math.floor(math.log2(n)))
    base = [2.0 ** (-8.0 * (i + 1) / p) for i in range(p)]
    extra = [2.0 ** (-4.0 * (2 * i + 1) / p) for i in range(n - p)]
    return [float(np.float32(s)) for s in base + extra]


def _adamod_kernel(c_ref, w_ref, b_ref, o_ref):
    c = c_ref[...]
    ca = (c * jax.nn.sigmoid(c)).astype(BF16)
    o_ref[...] = jnp.dot(ca, w_ref[...].astype(BF16), preferred_element_type=F32) + b_ref[...]


def _adamod(c_pad, ada_w, ada_b):
    ns, d, d3 = ada_w.shape
    rows = c_pad.shape[0]
    tn = 768
    return pl.pallas_call(
        _adamod_kernel,
        grid=(ns, d3 // tn),
        in_specs=[
            pl.BlockSpec((rows, d), lambda s, j: (0, 0)),
            pl.BlockSpec((None, d, tn), lambda s, j: (s, 0, j)),
            pl.BlockSpec((None, 1, tn), lambda s, j: (s, 0, j)),
        ],
        out_specs=pl.BlockSpec((None, rows, tn), lambda s, j: (s, 0, j)),
        out_shape=jax.ShapeDtypeStruct((ns, rows, d3), F32),
        compiler_params=_cp(("arbitrary", "arbitrary")),
        name="adamod",
    )(c_pad, ada_w, ada_b.reshape(ns, 1, d3))


def _modulate_kernel(x_ref, mod_ref, h_ref):
    h = x_ref[...] * (1.0 + mod_ref[1:2, :]) + mod_ref[0:1, :]
    h_ref[...] = h.astype(h_ref.dtype)


def _modulate(x2, mods, sub, seq):
    t, d = x2.shape
    tm = 512
    per_b = seq // tm
    return pl.pallas_call(
        _modulate_kernel,
        grid=(t // tm,),
        in_specs=[
            pl.BlockSpec((tm, d), lambda i: (i, 0)),
            pl.BlockSpec((None, None, 3, d), lambda i: (sub, i // per_b, 0, 0)),
        ],
        out_specs=pl.BlockSpec((tm, d), lambda i: (i, 0)),
        out_shape=jax.ShapeDtypeStruct((t, d), BF16),
        compiler_params=_cp(("arbitrary",)),
        name="modulate",
    )(x2, mods)


def _mm_kernel(a_ref, w_ref, b_ref, o_ref, *, act):
    acc = jnp.dot(a_ref[...], w_ref[...], preferred_element_type=F32) + b_ref[...]
    if act == "gelu":
        acc = 0.5 * acc * (1.0 + lax.erf(acc * np.float32(math.sqrt(0.5))))
    o_ref[...] = acc.astype(o_ref.dtype)


def _mm(a, w, b, *, act, tm, tn, out_dtype):
    m, k = a.shape
    n = w.shape[1]
    return pl.pallas_call(
        functools.partial(_mm_kernel, act=act),
        grid=(m // tm, n // tn),
        in_specs=[
            pl.BlockSpec((tm, k), lambda i, j: (i, 0)),
            pl.BlockSpec((k, tn), lambda i, j: (0, j)),
            pl.BlockSpec((1, tn), lambda i, j: (0, j)),
        ],
        out_specs=pl.BlockSpec((tm, tn), lambda i, j: (i, j)),
        out_shape=jax.ShapeDtypeStruct((m, n), out_dtype),
        compiler_params=_cp(("arbitrary", "arbitrary")),
        name="mm_" + (act or "bias"),
    )(a, w, b.reshape(1, n))


def _gate_kernel(u_ref, v_ref, g_ref, b_ref, ws_ref, bst_ref, o_ref, *, nchunk, gw):
    vn = _layer_norm(v_ref[...].astype(F32), g_ref[...], b_ref[...]).astype(BF16)
    row = lax.broadcasted_iota(jnp.int32, (CHUNK, CHUNK), 0)
    col = lax.broadcasted_iota(jnp.int32, (CHUNK, CHUNK), 1)
    causal = row >= col
    for g in range(GMLP_GROUPS):
        wsg = jnp.where(causal, ws_ref[g], 0.0).astype(BF16)
        bs = bst_ref[:, g:g + 1]
        for c in range(nchunk):
            rs = slice(c * CHUNK, (c + 1) * CHUNK)
            cs = slice(g * gw, (g + 1) * gw)
            s = jnp.dot(wsg, vn[rs, cs], preferred_element_type=F32) + bs
            o_ref[rs, cs] = (u_ref[rs, cs].astype(F32) * s).astype(o_ref.dtype)


def _gmlp_gate(z, ln_g, ln_b, w_s, b_s):
    t, n2 = z.shape
    half = n2 // 2
    tm = 2 * CHUNK
    gw = half // GMLP_GROUPS
    return pl.pallas_call(
        functools.partial(_gate_kernel, nchunk=tm // CHUNK, gw=gw),
        grid=(t // tm,),
        in_specs=[
            pl.BlockSpec((tm, half), lambda i: (i, 0)),
            pl.BlockSpec((tm, half), lambda i: (i, 1)),
            pl.BlockSpec((1, half), lambda i: (0, 0)),
            pl.BlockSpec((1, half), lambda i: (0, 0)),
            pl.BlockSpec((GMLP_GROUPS, CHUNK, CHUNK), lambda i: (0, 0, 0)),
            pl.BlockSpec((CHUNK, GMLP_GROUPS), lambda i: (0, 0)),
        ],
        out_specs=pl.BlockSpec((tm, half), lambda i: (i, 0)),
        out_shape=jax.ShapeDtypeStruct((t, half), BF16),
        compiler_params=_cp(("arbitrary",)),
        name="gmlp_gate",
    )(z, z, ln_g.reshape(1, half), ln_b.reshape(1, half), w_s, jnp.transpose(b_s))


def _attn_kernel(sink_ref, q_ref, kc_ref, vc_ref, kp_ref, vp_ref, o_ref, bias_ref, p_ref,
                 *, nb, n_kv, slopes):
    blk = ATT_BLOCK
    pairs = Q_PER_KV // 2
    i = pl.program_id(0)

    @pl.when(i == 0)
    def _build_bias():
        qi = lax.broadcasted_iota(jnp.int32, (blk, 2 * blk), 0)
        kj = lax.broadcasted_iota(jnp.int32, (blk, 2 * blk), 1)
        dist = qi + blk - kj
        valid = (dist >= 0) & (dist < blk)
        distf = dist.astype(F32)
        for g in range(n_kv):
            for p in range(pairs):
                for j in range(2):
                    slope = slopes[g * Q_PER_KV + 2 * p + j]
                    base = jnp.where(valid, -slope * distf, NEG_INF)
                    rs = slice(p * blk, (p + 1) * blk)
                    cs = slice(j * 2 * blk, (j + 1) * 2 * blk)
                    bias_ref[0, g, rs, cs] = base
                    bias_ref[1, g, rs, cs] = jnp.where(kj < blk, NEG_INF, base)

    first = (i % nb == 0).astype(jnp.int32)
    lane = lax.broadcasted_iota(jnp.int32, (2 * blk, LANES), 1)
    lo = lane < HEAD_DIM
    lo_q = lax.broadcasted_iota(jnp.int32, (blk, LANES), 1) < HEAD_DIM

    def blockdiag(prev_ref, cur_ref, g, scale):
        c, half = g // 2, g % 2
        cs = slice(c * LANES, (c + 1) * LANES)
        t2 = jnp.concatenate([prev_ref[:, cs], cur_ref[:, cs]], axis=0).astype(F32) * scale
        rolled = pltpu.roll(t2, HEAD_DIM, 1)
        if half == 0:
            a = jnp.where(lo, t2, 0.0)
            b = jnp.where(lo, 0.0, rolled)
        else:
            a = jnp.where(lo, rolled, 0.0)
            b = jnp.where(lo, 0.0, t2)
        return jnp.concatenate([a, b], axis=0).astype(BF16)

    for g in range(n_kv):
        kbd = blockdiag(kp_ref, kc_ref, g, HEAD_DIM ** -0.5)
        vbd = blockdiag(vp_ref, vc_ref, g, 1.0)
        qg = jnp.concatenate(
            [q_ref[:, (g * pairs + p) * LANES:(g * pairs + p + 1) * LANES] for p in range(pairs)], axis=0)
        s = lax.dot_general(qg, kbd, (((1,), (1,)), ((), ())), preferred_element_type=F32)
        s = s + bias_ref[first, g]
        inv = []
        for p in range(pairs):
            for j in range(2):
                rs = slice(p * blk, (p + 1) * blk)
                cs = slice(j * 2 * blk, (j + 1) * 2 * blk)
                sub = s[rs, cs]
                sink = sink_ref[g * Q_PER_KV + 2 * p + j]
                m = jnp.maximum(jnp.max(sub, axis=-1, keepdims=True), sink)
                e = jnp.exp(sub - m)
                denom = jnp.sum(e, axis=-1, keepdims=True) + jnp.exp(sink - m)
                p_ref[rs, cs] = e.astype(BF16)
                inv.append(1.0 / denom)
        o = jnp.dot(p_ref[...], vbd, preferred_element_type=F32)
        for p in range(pairs):
            scale = jnp.where(lo_q, inv[2 * p], inv[2 * p + 1])
            col = (g * pairs + p) * LANES
            o_ref[:, col:col + LANES] = (o[p * blk:(p + 1) * blk] * scale).astype(o_ref.dtype)


def _attention(qkv, sinks, seq):
    t, w = qkv.shape
    blk = ATT_BLOCK
    nb = seq // blk
    n_kv = (w // HEAD_DIM) // (Q_PER_KV + 2)
    dq = n_kv * Q_PER_KV * HEAD_DIM
    dkv = n_kv * HEAD_DIM
    kcol = dq // dkv
    pairs = Q_PER_KV // 2
    slopes = _alibi_slopes(n_kv * Q_PER_KV)

    def prev(i):
        return jnp.where(i % nb == 0, i, i - 1)

    return pl.pallas_call(
        functools.partial(_attn_kernel, nb=nb, n_kv=n_kv, slopes=slopes),
        grid=(t // blk,),
        in_specs=[
            pl.BlockSpec(memory_space=pltpu.SMEM),
            pl.BlockSpec((blk, dq), lambda i: (i, 0)),
            pl.BlockSpec((blk, dkv), lambda i: (i, kcol)),
            pl.BlockSpec((blk, dkv), lambda i: (i, kcol + 1)),
            pl.BlockSpec((blk, dkv), lambda i: (prev(i), kcol)),
            pl.BlockSpec((blk, dkv), lambda i: (prev(i), kcol + 1)),
        ],
        out_specs=pl.BlockSpec((blk, dq), lambda i: (i, 0)),
        out_shape=jax.ShapeDtypeStruct((t, dq), BF16),
        scratch_shapes=[
            pltpu.VMEM((2, n_kv, pairs * blk, 4 * blk), F32),
            pltpu.VMEM((pairs * blk, 4 * blk), BF16),
        ],
        compiler_params=_cp(("arbitrary",)),
        name="swa_attention",
    )(sinks, qkv, qkv, qkv, qkv, qkv)


def _resid_ln(y, x, mod_ref, lng_ref, lnb_ref):
    r = DEEPNORM_ALPHA * x + (1.0 + mod_ref[2:3, :]) * y
    return _layer_norm(r, lng_ref[...], lnb_ref[...])


def _proj_kernel(a_ref, w_ref, b_ref, x_ref, mod_ref, nmod_ref, lng_ref, lnb_ref, wr_ref,
                 xo_ref, ho_ref, lo_ref, acc_ref, *, nk):
    k = pl.program_id(1)

    @pl.when(k == 0)
    def _zero():
        acc_ref[...] = jnp.zeros_like(acc_ref)

    acc_ref[...] += jnp.dot(a_ref[...], w_ref[...], preferred_element_type=F32)

    @pl.when(k == nk - 1)
    def _finish():
        xn = _resid_ln(acc_ref[...] + b_ref[...], x_ref[...], mod_ref, lng_ref, lnb_ref)
        xo_ref[...] = xn
        h = xn * (1.0 + nmod_ref[1:2, :]) + nmod_ref[0:1, :]
        ho_ref[...] = h
        lo_ref[...] = jnp.dot(h, wr_ref[...], precision=lax.Precision.HIGHEST,
                              preferred_element_type=F32)


def _proj_resid_ln(a, w, b, x2, mods, sub, ln_g, ln_b, w_router_pad, seq):
    t, kdim = a.shape
    d = w.shape[1]
    tm = 256
    tk = min(kdim, 1024)
    nk = kdim // tk
    per_b = seq // tm
    ne = w_router_pad.shape[1]
    row = lambda i, k: (i, 0)
    const = lambda i, k: (0, 0)
    return pl.pallas_call(
        functools.partial(_proj_kernel, nk=nk),
        grid=(t // tm, nk),
        in_specs=[
            pl.BlockSpec((tm, tk), lambda i, k: (i, k)),
            pl.BlockSpec((tk, d), lambda i, k: (k, 0)),
            pl.BlockSpec((1, d), const),
            pl.BlockSpec((tm, d), row),
            pl.BlockSpec((None, None, 3, d), lambda i, k: (sub, i // per_b, 0, 0)),
            pl.BlockSpec((None, None, 3, d), lambda i, k: (sub + 1, i // per_b, 0, 0)),
            pl.BlockSpec((1, d), const),
            pl.BlockSpec((1, d), const),
            pl.BlockSpec((d, ne), const),
        ],
        out_specs=[
            pl.BlockSpec((tm, d), row),
            pl.BlockSpec((tm, d), row),
            pl.BlockSpec((tm, ne), row),
        ],
        out_shape=[
            jax.ShapeDtypeStruct((t, d), F32),
            jax.ShapeDtypeStruct((t, d), F32),
            jax.ShapeDtypeStruct((t, ne), F32),
        ],
        scratch_shapes=[pltpu.VMEM((tm, d), F32)],
        compiler_params=_cp(("arbitrary", "arbitrary")),
        name="proj_resid_ln",
    )(a, w, b.reshape(1, d), x2, mods, mods, ln_g.reshape(1, d), ln_b.reshape(1, d), w_router_pad)


def _row_copy(src_hbm, row, dst, slot, sem):
    return pltpu.make_async_copy(src_hbm.at[pl.ds(row, 1)], dst.at[pl.ds(slot, 1)], sem)


def _experts_kernel(blk_e_ref, blk_v_ref, tok_ref, h_hbm, wg_ref, wu_ref, wd_ref, o_ref, xbuf, sem):
    del blk_e_ref
    b = pl.program_id(0)
    bm = xbuf.shape[0]

    @pl.when(blk_v_ref[b] == 1)
    def _compute():
        def issue(r, carry):
            _row_copy(h_hbm, tok_ref[r], xbuf, r, sem).start()
            return carry

        lax.fori_loop(0, bm, issue, 0, unroll=8)

        def drain(r, carry):
            _row_copy(h_hbm, 0, xbuf, r, sem).wait()
            return carry

        lax.fori_loop(0, bm, drain, 0, unroll=8)
        xb = xbuf[...].astype(BF16)
        gt = jnp.dot(xb, wg_ref[...], preferred_element_type=F32)
        up = jnp.dot(xb, wu_ref[...], preferred_element_type=F32)
        act = (gt * jax.nn.sigmoid(gt) * up).astype(BF16)
        o_ref[...] = jnp.dot(act, wd_ref[...], preferred_element_type=F32)

    @pl.when(blk_v_ref[b] == 0)
    def _skip():
        o_ref[...] = jnp.zeros_like(o_ref)


def _experts(h32, row_tok, blk_e, blk_v, wg, wu, wd):
    t, d = h32.shape
    ne, _, ff = wg.shape
    bm = MOE_BM
    nblk = blk_e.shape[0]
    grid_spec = pltpu.PrefetchScalarGridSpec(
        num_scalar_prefetch=2,
        grid=(nblk,),
        in_specs=[
            pl.BlockSpec((bm,), lambda b, be, bv: (b,), memory_space=pltpu.SMEM),
            pl.BlockSpec(memory_space=pl.ANY),
            pl.BlockSpec((None, d, ff), lambda b, be, bv: (be[b], 0, 0)),
            pl.BlockSpec((None, d, ff), lambda b, be, bv: (be[b], 0, 0)),
            pl.BlockSpec((None, ff, d), lambda b, be, bv: (be[b], 0, 0)),
        ],
        out_specs=pl.BlockSpec((bm, d), lambda b, be, bv: (b, 0)),
        scratch_shapes=[pltpu.VMEM((bm, d), F32), pltpu.SemaphoreType.DMA(())],
    )
    return pl.pallas_call(
        _experts_kernel,
        grid_spec=grid_spec,
        out_shape=jax.ShapeDtypeStruct((nblk * bm, d), F32),
        compiler_params=_cp(("arbitrary",)),
        name="moe_experts",
    )(blk_e, blk_v, row_tok, h32, wg, wu, wd)


def _shared_kernel(h_ref, wg_ref, wu_ref, wd_ref, o_ref):
    hb = h_ref[...].astype(BF16)
    gt = jnp.dot(hb, wg_ref[...], preferred_element_type=F32)
    up = jnp.dot(hb, wu_ref[...], preferred_element_type=F32)
    act = (gt * jax.nn.sigmoid(gt) * up).astype(BF16)
    o_ref[...] = jnp.dot(act, wd_ref[...], preferred_element_type=F32)


def _shared(h32, wg, wu, wd):
    t, d = h32.shape
    ff = wg.shape[1]
    tm = 512
    const = lambda i: (0, 0)
    return pl.pallas_call(
        _shared_kernel,
        grid=(t // tm,),
        in_specs=[
            pl.BlockSpec((tm, d), lambda i: (i, 0)),
            pl.BlockSpec((d, ff), const),
            pl.BlockSpec((d, ff), const),
            pl.BlockSpec((ff, d), const),
        ],
        out_specs=pl.BlockSpec((tm, d), lambda i: (i, 0)),
        out_shape=jax.ShapeDtypeStruct((t, d), F32),
        compiler_params=_cp(("arbitrary",)),
        name="moe_shared",
    )(h32, wg, wu, wd)


def _combine_kernel(dest_ref, y_hbm, wts_ref, sh_ref, x_ref, mod_ref, nmod_ref, lng_ref, lnb_ref,
                    *rest, last):
    if last:
        xo_ref, ybuf, sem = rest
    else:
        xo_ref, ho_ref, ybuf, sem = rest
    tt = x_ref.shape[0]

    def issue(tok, carry):
        for k in range(TOP_K):
            _row_copy(y_hbm, dest_ref[tok * TOP_K + k], ybuf.at[k], tok, sem).start()
        return carry

    lax.fori_loop(0, tt, issue, 0)

    def drain(tok, carry):
        for k in range(TOP_K):
            _row_copy(y_hbm, 0, ybuf.at[k], tok, sem).wait()
        return carry

    lax.fori_loop(0, tt, drain, 0)

    routed = ybuf[0] * wts_ref[:, 0:1]
    for k in range(1, TOP_K):
        routed = routed + ybuf[k] * wts_ref[:, k:k + 1]
    xn = _resid_ln(routed + sh_ref[...], x_ref[...], mod_ref, lng_ref, lnb_ref)
    xo_ref[...] = xn
    if not last:
        ho_ref[...] = (xn * (1.0 + nmod_ref[1:2, :]) + nmod_ref[0:1, :]).astype(ho_ref.dtype)


def _combine(ys, dest, wts, shared, x2, mods, sub, ln_g, ln_b, seq, last):
    t, d = x2.shape
    tt = COMB_TT
    per_b = seq // tt
    nsub = sub if last else sub + 1
    row = lambda i: (i, 0)
    const = lambda i: (0, 0)
    outs = pl.pallas_call(
        functools.partial(_combine_kernel, last=last),
        grid=(t // tt,),
        in_specs=[
            pl.BlockSpec((tt * TOP_K,), lambda i: (i,), memory_space=pltpu.SMEM),
            pl.BlockSpec(memory_space=pl.ANY),
            pl.BlockSpec((tt, TOP_K), row),
            pl.BlockSpec((tt, d), row),
            pl.BlockSpec((tt, d), row),
            pl.BlockSpec((None, None, 3, d), lambda i: (sub, i // per_b, 0, 0)),
            pl.BlockSpec((None, None, 3, d), lambda i: (nsub, i // per_b, 0, 0)),
            pl.BlockSpec((1, d), const),
            pl.BlockSpec((1, d), const),
        ],
        out_specs=[pl.BlockSpec((tt, d), row)] * (1 if last else 2),
        out_shape=[jax.ShapeDtypeStruct((t, d), F32)] + ([] if last else [jax.ShapeDtypeStruct((t, d), BF16)]),
        scratch_shapes=[pltpu.VMEM((TOP_K, tt, d), F32), pltpu.SemaphoreType.DMA(())],
        compiler_params=_cp(("arbitrary",)),
        name="moe_combine",
    )(dest, ys, wts, shared, x2, mods, mods, ln_g.reshape(1, d), ln_b.reshape(1, d))
    return (outs[0], None) if last else (outs[0], outs[1])


def _route(logits, router_bias):
    t = logits.shape[0]
    scores = jax.nn.sigmoid(logits)
    sel = scores + router_bias.astype(F32)
    gsz = N_EXPERTS // N_EXPERT_GROUPS
    grp_score = jnp.sum(lax.top_k(sel.reshape(t, N_EXPERT_GROUPS, gsz), 2)[0], axis=-1)
    _, top_groups = lax.top_k(grp_score, TOPK_GROUPS)
    gmask = jnp.any(top_groups[:, :, None] == jnp.arange(N_EXPERT_GROUPS)[None, None, :], axis=1)
    sel = jnp.where(jnp.repeat(gmask, gsz, axis=1), sel, NEG_INF)
    _, eidx = lax.top_k(sel, TOP_K)
    w = jnp.take_along_axis(scores, eidx, axis=1)
    w = w / jnp.sum(w, axis=-1, keepdims=True) * ROUTED_SCALE

    bm = MOE_BM
    onehot = jnp.any(eidx[:, :, None] == jnp.arange(N_EXPERTS)[None, None, :], axis=1).astype(jnp.int32)
    counts = jnp.sum(onehot, axis=0)
    padded = (counts + bm - 1) // bm * bm
    pends = jnp.cumsum(padded)
    pstarts = pends - padded
    rank = jnp.cumsum(onehot, axis=0) - onehot
    dest = pstarts[eidx] + jnp.take_along_axis(rank, eidx, axis=1)
    nblk = (t * TOP_K) // bm + N_EXPERTS
    tok = jnp.broadcast_to(jnp.arange(t, dtype=jnp.int32)[:, None], (t, TOP_K))
    row_tok = jnp.zeros((nblk * bm,), jnp.int32).at[dest.reshape(-1)].set(tok.reshape(-1))
    blk_start = jnp.arange(nblk, dtype=jnp.int32) * bm
    blk_e = jnp.minimum(jnp.sum(blk_start[:, None] >= pends[None, :], axis=1), N_EXPERTS - 1)
    blk_v = (blk_start < pends[-1]).astype(jnp.int32)
    return dest.reshape(-1).astype(jnp.int32), w, row_tok, blk_e.astype(jnp.int32), blk_v


def kernel(x, c, ada_w, ada_b, ln_g, ln_b, gm_w_in, gm_b_in, gm_v_ln_g, gm_v_ln_b, gm_w_s, gm_b_s,
           gm_w_out, gm_b_out, at_w_qkv, at_b_qkv, at_sinks, at_w_o, at_b_o, moe_w_router,
           moe_router_bias, moe_w_gate, moe_w_up, moe_w_down, moe_ws_gate, moe_ws_up, moe_ws_down):
    bsz, seq, d = x.shape
    t = bsz * seq
    depth = ada_w.shape[0]
    x2 = x.reshape(t, d)

    rows = 8
    c_pad = jnp.zeros((rows, d), F32).at[:bsz].set(c)
    mods = _adamod(c_pad, ada_w.reshape(depth * 2, d, 3 * d), ada_b.reshape(depth * 2, 3 * d))
    mods = mods.reshape(depth * 2, rows, 3, d)

    h = _modulate(x2, mods, 0, seq)
    for i in range(depth):
        j = i // 2
        wr = jnp.zeros((d, LANES), F32).at[:, :N_EXPERTS].set(moe_w_router[i])
        if i % 2 == 0:
            z = _mm(h, gm_w_in[j].astype(BF16), gm_b_in[j], act="gelu", tm=512, tn=1024, out_dtype=BF16)
            a = _gmlp_gate(z, gm_v_ln_g[j], gm_v_ln_b[j], gm_w_s[j], gm_b_s[j])
            x2, h32, logits = _proj_resid_ln(a, gm_w_out[j].astype(BF16), gm_b_out[j], x2, mods, 2 * i,
                                             ln_g[i, 0], ln_b[i, 0], wr, seq)
        else:
            qkv = _mm(h, at_w_qkv[j].astype(BF16), at_b_qkv[j], act=None, tm=512, tn=1280, out_dtype=BF16)
            o = _attention(qkv, at_sinks[j], seq)
            x2, h32, logits = _proj_resid_ln(o, at_w_o[j].astype(BF16), at_b_o[j], x2, mods, 2 * i,
                                             ln_g[i, 0], ln_b[i, 0], wr, seq)
        dest, wts, row_tok, blk_e, blk_v = _route(logits[:, :N_EXPERTS], moe_router_bias[i])
        ys = _experts(h32, row_tok, blk_e, blk_v, moe_w_gate[i].astype(BF16), moe_w_up[i].astype(BF16),
                      moe_w_down[i].astype(BF16))
        sh = _shared(h32, moe_ws_gate[i].astype(BF16), moe_ws_up[i].astype(BF16), moe_ws_down[i].astype(BF16))
        x2, h = _combine(ys, dest, wts, sh, x2, mods, 2 * i + 1, ln_g[i, 1], ln_b[i, 1], seq,
                         last=(i == depth - 1))
    return x2.reshape(bsz, seq, d)
```

```python
import functools
import math

import numpy as np
import jax
import jax.numpy as jnp
from jax import lax
from jax.experimental import pallas as pl
from jax.experimental.pallas import tpu as pltpu

F32 = jnp.float32
BF16 = jnp.bfloat16

DEPTH = 4
CHUNK = 128
GMLP_GROUPS = 8
HEAD_DIM = 64
Q_PER_KV = 8
ATT_BLOCK = 128
N_EXPERTS = 64
TOP_K = 8
N_EXPERT_GROUPS = 8
TOPK_GROUPS = 4
ROUTED_SCALE = 2.5
DEEPNORM_ALPHA = (2.0 * DEPTH) ** 0.25
LN_EPS = 1e-5
NEG_INF = -1e30

LANES = 128
VMEM_LIMIT = 56 * 1024 * 1024

MOE_BM = 256
COMB_TT = 128
DISP_TD = 256


def _cp(sem, vmem=VMEM_LIMIT):
    return pltpu.CompilerParams(dimension_semantics=sem, vmem_limit_bytes=vmem)


def _layer_norm(x, g, b):
    mu = jnp.mean(x, axis=-1, keepdims=True)
    xc = x - mu
    var = jnp.mean(xc * xc, axis=-1, keepdims=True)
    return xc * lax.rsqrt(var + LN_EPS) * g + b


def _alibi_slopes(n):
    p = 2 ** int(math.floor(math.log2(n)))
    base = [2.0 ** (-8.0 * (i + 1) / p) for i in range(p)]
    extra = [2.0 ** (-4.0 * (2 * i + 1) / p) for i in range(n - p)]
    return [float(np.float32(s)) for s in base + extra]


def _adamod_kernel(c_ref, w_ref, b_ref, o_ref):
    c = c_ref[...]
    ca = (c * jax.nn.sigmoid(c)).astype(BF16)
    o_ref[...] = jnp.dot(ca, w_ref[...].astype(BF16), preferred_element_type=F32) + b_ref[...]


def _adamod(c_pad, ada_w, ada_b):
    ns, d, d3 = ada_w.shape
    rows = c_pad.shape[0]
    tn = 768
    return pl.pallas_call(
        _adamod_kernel,
        grid=(ns, d3 // tn),
        in_specs=[
            pl.BlockSpec((rows, d), lambda s, j: (0, 0)),
            pl.BlockSpec((None, d, tn), lambda s, j: (s, 0, j)),
            pl.BlockSpec((None, 1, tn), lambda s, j: (s, 0, j)),
        ],
        out_specs=pl.BlockSpec((None, rows, tn), lambda s, j: (s, 0, j)),
        out_shape=jax.ShapeDtypeStruct((ns, rows, d3), F32),
        compiler_params=_cp(("arbitrary", "arbitrary")),
        name="adamod",
    )(c_pad, ada_w, ada_b.reshape(ns, 1, d3))


def _modulate_kernel(x_ref, mod_ref, h_ref):
    h = x_ref[...] * (1.0 + mod_ref[1:2, :]) + mod_ref[0:1, :]
    h_ref[...] = h.astype(h_ref.dtype)


def _modulate(x2, mods, sub, seq):
    t, d = x2.shape
    tm = 512
    per_b = seq // tm
    return pl.pallas_call(
        _modulate_kernel,
        grid=(t // tm,),
        in_specs=[
            pl.BlockSpec((tm, d), lambda i: (i, 0)),
            pl.BlockSpec((None, None, 3, d), lambda i: (sub, i // per_b, 0, 0)),
        ],
        out_specs=pl.BlockSpec((tm, d), lambda i: (i, 0)),
        out_shape=jax.ShapeDtypeStruct((t, d), BF16),
        compiler_params=_cp(("arbitrary",)),
        name="modulate",
    )(x2, mods)


def _mm_kernel(a_ref, w_ref, b_ref, o_ref, *, act):
    acc = jnp.dot(a_ref[...], w_ref[...], preferred_element_type=F32) + b_ref[...]
    if act == "gelu":
        acc = 0.5 * acc * (1.0 + lax.erf(acc * np.float32(math.sqrt(0.5))))
    o_ref[...] = acc.astype(o_ref.dtype)


def _mm(a, w, b, *, act, tm, tn, out_dtype):
    m, k = a.shape
    n = w.shape[1]
    return pl.pallas_call(
        functools.partial(_mm_kernel, act=act),
        grid=(m // tm, n // tn),
        in_specs=[
            pl.BlockSpec((tm, k), lambda i, j: (i, 0)),
            pl.BlockSpec((k, tn), lambda i, j: (0, j)),
            pl.BlockSpec((1, tn), lambda i, j: (0, j)),
        ],
        out_specs=pl.BlockSpec((tm, tn), lambda i, j: (i, j)),
        out_shape=jax.ShapeDtypeStruct((m, n), out_dtype),
        compiler_params=_cp(("arbitrary", "arbitrary")),
        name="mm_" + (act or "bias"),
    )(a, w, b.reshape(1, n))


def _gate_kernel(u_ref, v_ref, g_ref, b_ref, ws_ref, bst_ref, o_ref, *, nchunk, gw):
    vn = _layer_norm(v_ref[...].astype(F32), g_ref[...], b_ref[...]).astype(BF16)
    row = lax.broadcasted_iota(jnp.int32, (CHUNK, CHUNK), 0)
    col = lax.broadcasted_iota(jnp.int32, (CHUNK, CHUNK), 1)
    causal = row >= col
    for g in range(GMLP_GROUPS):
        wsg = jnp.where(causal, ws_ref[g], 0.0).astype(BF16)
        bs = bst_ref[:, g:g + 1]
        for c in range(nchunk):
            rs = slice(c * CHUNK, (c + 1) * CHUNK)
            cs = slice(g * gw, (g + 1) * gw)
            s = jnp.dot(wsg, vn[rs, cs], preferred_element_type=F32) + bs
            o_ref[rs, cs] = (u_ref[rs, cs].astype(F32) * s).astype(o_ref.dtype)


def _gmlp_gate(z, ln_g, ln_b, w_s, b_s):
    t, n2 = z.shape
    half = n2 // 2
    tm = 2 * CHUNK
    gw = half // GMLP_GROUPS
    return pl.pallas_call(
        functools.partial(_gate_kernel, nchunk=tm // CHUNK, gw=gw),
        grid=(t // tm,),
        in_specs=[
            pl.BlockSpec((tm, half), lambda i: (i, 0)),
            pl.BlockSpec((tm, half), lambda i: (i, 1)),
            pl.BlockSpec((1, half), lambda i: (0, 0)),
            pl.BlockSpec((1, half), lambda i: (0, 0)),
            pl.BlockSpec((GMLP_GROUPS, CHUNK, CHUNK), lambda i: (0, 0, 0)),
            pl.BlockSpec((CHUNK, GMLP_GROUPS), lambda i: (0, 0)),
        ],
        out_specs=pl.BlockSpec((tm, half), lambda i: (i, 0)),
        out_shape=jax.ShapeDtypeStruct((t, half), BF16),
        compiler_params=_cp(("arbitrary",)),
        name="gmlp_gate",
    )(z, z, ln_g.reshape(1, half), ln_b.reshape(1, half), w_s, jnp.transpose(b_s))


def _attn_kernel(sink_ref, q_ref, kc_ref, vc_ref, kp_ref, vp_ref, o_ref, bias_ref, p_ref,
                 *, nb, n_kv, slopes):
    blk = ATT_BLOCK
    pairs = Q_PER_KV // 2
    i = pl.program_id(0)

    @pl.when(i == 0)
    def _build_bias():
        qi = lax.broadcasted_iota(jnp.int32, (blk, 2 * blk), 0)
        kj = lax.broadcasted_iota(jnp.int32, (blk, 2 * blk), 1)
        dist = qi + blk - kj
        valid = (dist >= 0) & (dist < blk)
        distf = dist.astype(F32)
        for g in range(n_kv):
            for p in range(pairs):
                for j in range(2):
                    slope = slopes[g * Q_PER_KV + 2 * p + j]
                    base = jnp.where(valid, -slope * distf, NEG_INF)
                    rs = slice(p * blk, (p + 1) * blk)
                    cs = slice(j * 2 * blk, (j + 1) * 2 * blk)
                    bias_ref[0, g, rs, cs] = base
                    bias_ref[1, g, rs, cs] = jnp.where(kj < blk, NEG_INF, base)

    first = (i % nb == 0).astype(jnp.int32)
    lane = lax.broadcasted_iota(jnp.int32, (2 * blk, LANES), 1)
    lo = lane < HEAD_DIM
    lo_q = lax.broadcasted_iota(jnp.int32, (blk, LANES), 1) < HEAD_DIM

    def blockdiag(prev_ref, cur_ref, g, scale):
        c, half = g // 2, g % 2
        cs = slice(c * LANES, (c + 1) * LANES)
        t2 = jnp.concatenate([prev_ref[:, cs], cur_ref[:, cs]], axis=0).astype(F32) * scale
        rolled = pltpu.roll(t2, HEAD_DIM, 1)
        if half == 0:
            a = jnp.where(lo, t2, 0.0)
            b = jnp.where(lo, 0.0, rolled)
        else:
            a = jnp.where(lo, rolled, 0.0)
            b = jnp.where(lo, 0.0, t2)
        return jnp.concatenate([a, b], axis=0).astype(BF16)

    for g in range(n_kv):
        kbd = blockdiag(kp_ref, kc_ref, g, HEAD_DIM ** -0.5)
        vbd = blockdiag(vp_ref, vc_ref, g, 1.0)
        qg = jnp.concatenate(
            [q_ref[:, (g * pairs + p) * LANES:(g * pairs + p + 1) * LANES] for p in range(pairs)], axis=0)
        s = lax.dot_general(qg, kbd, (((1,), (1,)), ((), ())), preferred_element_type=F32)
        s = s + bias_ref[first, g]
        inv = []
        for p in range(pairs):
            for j in range(2):
                rs = slice(p * blk, (p + 1) * blk)
                cs = slice(j * 2 * blk, (j + 1) * 2 * blk)
                sub = s[rs, cs]
                sink = sink_ref[g * Q_PER_KV + 2 * p + j]
                m = jnp.maximum(jnp.max(sub, axis=-1, keepdims=True), sink)
                e = jnp.exp(sub - m)
                denom = jnp.sum(e, axis=-1, keepdims=True) + jnp.exp(sink - m)
                p_ref[rs, cs] = e.astype(BF16)
                inv.append(1.0 / denom)
        o = jnp.dot(p_ref[...], vbd, preferred_element_type=F32)
        for p in range(pairs):
            scale = jnp.where(lo_q, inv[2 * p], inv[2 * p + 1])
            col = (g * pairs + p) * LANES
            o_ref[:, col:col + LANES] = (o[p * blk:(p + 1) * blk] * scale).astype(o_ref.dtype)


def _attention(qkv, sinks, seq):
    t, w = qkv.shape
    blk = ATT_BLOCK
    nb = seq // blk
    n_kv = (w // HEAD_DIM) // (Q_PER_KV + 2)
    dq = n_kv * Q_PER_KV * HEAD_DIM
    dkv = n_kv * HEAD_DIM
    kcol = dq // dkv
    pairs = Q_PER_KV // 2
    slopes = _alibi_slopes(n_kv * Q_PER_KV)

    def prev(i):
        return jnp.where(i % nb == 0, i, i - 1)

    return pl.pallas_call(
        functools.partial(_attn_kernel, nb=nb, n_kv=n_kv, slopes=slopes),
        grid=(t // blk,),
        in_specs=[
            pl.BlockSpec(memory_space=pltpu.SMEM),
            pl.BlockSpec((blk, dq), lambda i: (i, 0)),
            pl.BlockSpec((blk, dkv), lambda i: (i, kcol)),
            pl.BlockSpec((blk, dkv), lambda i: (i, kcol + 1)),
            pl.BlockSpec((blk, dkv), lambda i: (prev(i), kcol)),
            pl.BlockSpec((blk, dkv), lambda i: (prev(i), kcol + 1)),
        ],
        out_specs=pl.BlockSpec((blk, dq), lambda i: (i, 0)),
        out_shape=jax.ShapeDtypeStruct((t, dq), BF16),
        scratch_shapes=[
            pltpu.VMEM((2, n_kv, pairs * blk, 4 * blk), F32),
            pltpu.VMEM((pairs * blk, 4 * blk), BF16),
        ],
        compiler_params=_cp(("arbitrary",)),
        name="swa_attention",
    )(sinks, qkv, qkv, qkv, qkv, qkv)


def _pack_rows(h):
    half = h.shape[1] // 2
    bits = lax.bitcast_convert_type(h.astype(BF16).astype(F32), jnp.uint32)
    return (bits[:, :half] >> 16) | (bits[:, half:] & jnp.uint32(0xFFFF0000))


def _unpack_rows(p):
    lo = lax.bitcast_convert_type(p << 16, F32).astype(BF16)
    hi = lax.bitcast_convert_type(p & jnp.uint32(0xFFFF0000), F32).astype(BF16)
    return lo, hi


def _swiglu_packed(p, wg, wu, wd):
    lo, hi = _unpack_rows(p)
    half = p.shape[1]
    gt = (jnp.dot(lo, wg[:half], preferred_element_type=F32)
          + jnp.dot(hi, wg[half:], preferred_element_type=F32))
    up = (jnp.dot(lo, wu[:half], preferred_element_type=F32)
          + jnp.dot(hi, wu[half:], preferred_element_type=F32))
    act = (gt * jax.nn.sigmoid(gt) * up).astype(BF16)
    return jnp.dot(act, wd, preferred_element_type=F32)


def _resid_ln(y, x, mod_ref, lng_ref, lnb_ref):
    r = DEEPNORM_ALPHA * x + (1.0 + mod_ref[2:3, :]) * y
    return _layer_norm(r, lng_ref[...], lnb_ref[...])


def _proj_kernel(a_ref, w_ref, b_ref, x_ref, mod_ref, nmod_ref, lng_ref, lnb_ref, wrt_ref,
                 xo_ref, hp_ref, lt_ref, acc_ref, *, nk, slab):
    k = pl.program_id(1)

    @pl.when(k == 0)
    def _zero():
        acc_ref[...] = jnp.zeros_like(acc_ref)

    acc_ref[...] += jnp.dot(a_ref[...], w_ref[...], preferred_element_type=F32)

    @pl.when(k == nk - 1)
    def _finish():
        for r0 in range(0, acc_ref.shape[0], slab):
            rs = slice(r0, r0 + slab)
            xn = _resid_ln(acc_ref[rs, :] + b_ref[...], x_ref[rs, :], mod_ref, lng_ref, lnb_ref)
            xo_ref[rs, :] = xn
            h = xn * (1.0 + nmod_ref[1:2, :]) + nmod_ref[0:1, :]
            hp_ref[rs, :] = _pack_rows(h)
            lt_ref[:, rs] = lax.dot_general(wrt_ref[...], h, (((1,), (1,)), ((), ())),
                                            precision=lax.Precision.HIGHEST,
                                            preferred_element_type=F32)


def _proj_resid_ln(a, w, b, x2, mods, sub, ln_g, ln_b, w_router_t, seq):
    t, kdim = a.shape
    d = w.shape[1]
    tm = 512
    tk = min(kdim, 1024)
    nk = kdim // tk
    per_b = seq // tm
    ne = w_router_t.shape[0]
    row = lambda i, k: (i, 0)
    const = lambda i, k: (0, 0)
    return pl.pallas_call(
        functools.partial(_proj_kernel, nk=nk, slab=128),
        grid=(t // tm, nk),
        in_specs=[
            pl.BlockSpec((tm, tk), lambda i, k: (i, k)),
            pl.BlockSpec((tk, d), lambda i, k: (k, 0)),
            pl.BlockSpec((1, d), const),
            pl.BlockSpec((tm, d), row),
            pl.BlockSpec((None, None, 3, d), lambda i, k: (sub, i // per_b, 0, 0)),
            pl.BlockSpec((None, None, 3, d), lambda i, k: (sub + 1, i // per_b, 0, 0)),
            pl.BlockSpec((1, d), const),
            pl.BlockSpec((1, d), const),
            pl.BlockSpec((ne, d), const),
        ],
        out_specs=[
            pl.BlockSpec((tm, d), row),
            pl.BlockSpec((tm, d // 2), row),
            pl.BlockSpec((ne, tm), lambda i, k: (0, i)),
        ],
        out_shape=[
            jax.ShapeDtypeStruct((t, d), F32),
            jax.ShapeDtypeStruct((t, d // 2), jnp.uint32),
            jax.ShapeDtypeStruct((ne, t), F32),
        ],
        scratch_shapes=[pltpu.VMEM((tm, d), F32)],
        compiler_params=_cp(("arbitrary", "arbitrary")),
        name="proj_resid_ln",
    )(a, w, b.reshape(1, d), x2, mods, mods, ln_g.reshape(1, d), ln_b.reshape(1, d), w_router_t)


def _route_kernel(lg_ref, bias_ref, eidx_ref, rank_ref, w_ref, cnt_ref, tri_ref, carry_ref):
    i = pl.program_id(0)
    ne, tt = lg_ref.shape
    gsz = ne // N_EXPERT_GROUPS
    neg = -jnp.inf

    @pl.when(i == 0)
    def _init():
        r = lax.broadcasted_iota(jnp.int32, (tt, tt), 0)
        c = lax.broadcasted_iota(jnp.int32, (tt, tt), 1)
        tri_ref[...] = jnp.where(r < c, 1.0, 0.0).astype(BF16)
        carry_ref[...] = jnp.zeros_like(carry_ref)

    def amax(v, ids, limit):
        m = jnp.max(v, axis=0, keepdims=True)
        idx = jnp.min(jnp.where(v == m, ids, limit), axis=0, keepdims=True)
        return m, idx

    s = jax.nn.sigmoid(lg_ref[...])
    sel = s + bias_ref[:, 0:1]

    ids_g = lax.broadcasted_iota(jnp.int32, (gsz, tt), 0).astype(F32)
    gscore = []
    for g in range(N_EXPERT_GROUPS):
        xg = sel[g * gsz:(g + 1) * gsz]
        m1, i1 = amax(xg, ids_g, float(gsz))
        m2 = jnp.max(jnp.where(ids_g == i1, neg, xg), axis=0, keepdims=True)
        gscore.append(m1 + m2)
    work = jnp.concatenate(gscore, axis=0)
    ids_n = lax.broadcasted_iota(jnp.int32, (N_EXPERT_GROUPS, tt), 0).astype(F32)
    gsel = jnp.zeros((N_EXPERT_GROUPS, tt), F32)
    for _ in range(TOPK_GROUPS):
        _, ig = amax(work, ids_n, float(N_EXPERT_GROUPS))
        pick = ids_n == ig
        gsel = jnp.where(pick, 1.0, gsel)
        work = jnp.where(pick, neg, work)
    work = jnp.concatenate(
        [jnp.where(gsel[g:g + 1] > 0.5, sel[g * gsz:(g + 1) * gsz], NEG_INF)
         for g in range(N_EXPERT_GROUPS)], axis=0)

    ids_e = lax.broadcasted_iota(jnp.int32, (ne, tt), 0).astype(F32)
    chosen = jnp.zeros((ne, tt), F32)
    picked, svals = [], []
    for _ in range(TOP_K):
        _, ik = amax(work, ids_e, float(ne))
        pick = ids_e == ik
        chosen = jnp.where(pick, 1.0, chosen)
        work = jnp.where(pick, neg, work)
        picked.append(ik)
        svals.append(jnp.sum(jnp.where(pick, s, 0.0), axis=0, keepdims=True))
    denom = svals[0]
    for k in range(1, TOP_K):
        denom = denom + svals[k]

    prefix = jnp.dot(chosen.astype(BF16), tri_ref[...], preferred_element_type=F32)
    rank_dense = prefix + carry_ref[:, 0:1]
    carry_ref[...] = carry_ref[...] + jnp.sum(chosen, axis=1, keepdims=True)
    cnt_ref[...] = carry_ref[...]
    for k in range(TOP_K):
        pick = ids_e == picked[k]
        eidx_ref[k:k + 1, :] = picked[k].astype(jnp.int32)
        rank_ref[k:k + 1, :] = jnp.sum(jnp.where(pick, rank_dense, 0.0), axis=0,
                                       keepdims=True).astype(jnp.int32)
        w_ref[k:k + 1, :] = svals[k] / denom * ROUTED_SCALE


def _route(logits_t, router_bias):
    ne, t = logits_t.shape
    tt = 1024
    bias = jnp.broadcast_to(router_bias.astype(F32)[:, None], (ne, LANES))
    tok = lambda i: (0, i)
    const = lambda i: (0, 0)
    return pl.pallas_call(
        _route_kernel,
        grid=(t // tt,),
        in_specs=[pl.BlockSpec((ne, tt), tok), pl.BlockSpec((ne, LANES), const)],
        out_specs=[
            pl.BlockSpec((TOP_K, tt), tok),
            pl.BlockSpec((TOP_K, tt), tok),
            pl.BlockSpec((TOP_K, tt), tok),
            pl.BlockSpec((ne, LANES), const),
        ],
        out_shape=[
            jax.ShapeDtypeStruct((TOP_K, t), jnp.int32),
            jax.ShapeDtypeStruct((TOP_K, t), jnp.int32),
            jax.ShapeDtypeStruct((TOP_K, t), F32),
            jax.ShapeDtypeStruct((ne, LANES), F32),
        ],
        scratch_shapes=[pltpu.VMEM((tt, tt), BF16), pltpu.VMEM((ne, LANES), F32)],
        compiler_params=_cp(("arbitrary",)),
        name="moe_route",
    )(logits_t, bias)


def _dest_kernel(pstart_ref, eidx_ref, rank_ref, dest_ref):
    e = eidx_ref[...]
    base = jnp.zeros_like(e)
    for x in range(N_EXPERTS):
        base = jnp.where(e == x, pstart_ref[x], base)
    dest_ref[...] = rank_ref[...] + base


def _dest_rows(pstart, eidx, rank):
    return pl.pallas_call(
        _dest_kernel,
        in_specs=[pl.BlockSpec(memory_space=pltpu.SMEM),
                  pl.BlockSpec(memory_space=pltpu.VMEM),
                  pl.BlockSpec(memory_space=pltpu.VMEM)],
        out_specs=pl.BlockSpec(memory_space=pltpu.VMEM),
        out_shape=jax.ShapeDtypeStruct(eidx.shape, jnp.int32),
        name="moe_dest",
    )(pstart, eidx, rank)


def _dispatch_kernel(pend_ref, has_ref, dest_ref, hp_ref, xs_hbm, zbuf, sem, zsem):
    i = pl.program_id(0)
    td = hp_ref.shape[0]
    bm = zbuf.shape[0]

    @pl.when(i == 0)
    def _zero_tails():
        zbuf[...] = jnp.zeros_like(zbuf)

        def tail(e):
            start_row = pl.multiple_of(pend_ref[e] - bm, bm)
            return pltpu.make_async_copy(zbuf, xs_hbm.at[pl.ds(start_row, bm)], zsem)

        def start(e, carry):
            @pl.when(has_ref[e] == 1)
            def _():
                tail(e).start()
            return carry

        def wait(e, carry):
            @pl.when(has_ref[e] == 1)
            def _():
                tail(e).wait()
            return carry

        lax.fori_loop(0, N_EXPERTS, start, 0)
        lax.fori_loop(0, N_EXPERTS, wait, 0)

    def row(tok, k):
        return pltpu.make_async_copy(hp_ref.at[pl.ds(tok, 1)],
                                     xs_hbm.at[pl.ds(dest_ref[k * td + tok], 1)], sem)

    def start_rows(tok, carry):
        for k in range(TOP_K):
            row(tok, k).start(priority=k % 2)
        return carry

    def wait_rows(tok, carry):
        for k in range(TOP_K):
            row(tok, k).wait()
        return carry

    lax.fori_loop(0, td, start_rows, 0)
    lax.fori_loop(0, td, wait_rows, 0)


def _dispatch(hp, dest_tiles, pends, has, n_rows, td):
    t, dh = hp.shape
    grid_spec = pltpu.PrefetchScalarGridSpec(
        num_scalar_prefetch=2,
        grid=(t // td,),
        in_specs=[
            pl.BlockSpec((TOP_K * td,), lambda i, pe, ha: (i,), memory_space=pltpu.SMEM),
            pl.BlockSpec((td, dh), lambda i, pe, ha: (i, 0)),
        ],
        out_specs=pl.BlockSpec(memory_space=pl.ANY),
        scratch_shapes=[pltpu.VMEM((MOE_BM, dh), jnp.uint32), pltpu.SemaphoreType.DMA(()),
                        pltpu.SemaphoreType.DMA(())],
    )
    return pl.pallas_call(
        _dispatch_kernel,
        grid_spec=grid_spec,
        out_shape=jax.ShapeDtypeStruct((n_rows, dh), jnp.uint32),
        compiler_params=_cp(("arbitrary",)),
        name="moe_dispatch",
    )(pends, has, dest_tiles, hp)


def _experts_kernel(blk_e_ref, blk_v_ref, xs_ref, wg_ref, wu_ref, wd_ref, o_ref, wgb, wub, wdb):
    b = pl.program_id(0)
    valid = blk_v_ref[b] == 1
    changed = (b == 0) | (blk_e_ref[b] != blk_e_ref[jnp.maximum(b - 1, 0)])

    @pl.when(valid & changed)
    def _cast_weights():
        wgb[...] = wg_ref[...].astype(BF16)
        wub[...] = wu_ref[...].astype(BF16)
        wdb[...] = wd_ref[...].astype(BF16)

    @pl.when(valid)
    def _compute():
        o_ref[...] = _swiglu_packed(xs_ref[...], wgb[...], wub[...], wdb[...])

    @pl.when(jnp.logical_not(valid))
    def _skip():
        o_ref[...] = jnp.zeros_like(o_ref)


def _experts(xs, blk_e, blk_v, wg, wu, wd, layer):
    _, ne, d, ff = wg.shape
    bm = MOE_BM
    nblk = blk_e.shape[0]
    wmap = lambda b, be, bv: (layer, be[b], 0, 0)
    grid_spec = pltpu.PrefetchScalarGridSpec(
        num_scalar_prefetch=2,
        grid=(nblk,),
        in_specs=[
            pl.BlockSpec((bm, d // 2), lambda b, be, bv: (jnp.where(bv[b] == 1, b, 0), 0)),
            pl.BlockSpec((None, None, d, ff), wmap),
            pl.BlockSpec((None, None, d, ff), wmap),
            pl.BlockSpec((None, None, ff, d), wmap),
        ],
        out_specs=pl.BlockSpec((bm, d), lambda b, be, bv: (b, 0)),
        scratch_shapes=[pltpu.VMEM((d, ff), BF16), pltpu.VMEM((d, ff), BF16), pltpu.VMEM((ff, d), BF16)],
    )
    return pl.pallas_call(
        _experts_kernel,
        grid_spec=grid_spec,
        out_shape=jax.ShapeDtypeStruct((nblk * bm, d), F32),
        compiler_params=_cp(("arbitrary",)),
        name="moe_experts",
    )(blk_e, blk_v, xs, wg, wu, wd)


def _shared_kernel(hp_ref, wg_ref, wu_ref, wd_ref, o_ref):
    o_ref[...] = _swiglu_packed(hp_ref[...], wg_ref[...], wu_ref[...], wd_ref[...])


def _shared(hp, wg, wu, wd):
    t, dh = hp.shape
    d, ff = wg.shape
    tm = 512
    const = lambda i: (0, 0)
    return pl.pallas_call(
        _shared_kernel,
        grid=(t // tm,),
        in_specs=[
            pl.BlockSpec((tm, dh), lambda i: (i, 0)),
            pl.BlockSpec((d, ff), const),
            pl.BlockSpec((d, ff), const),
            pl.BlockSpec((ff, d), const),
        ],
        out_specs=pl.BlockSpec((tm, d), lambda i: (i, 0)),
        out_shape=jax.ShapeDtypeStruct((t, d), F32),
        compiler_params=_cp(("arbitrary",)),
        name="moe_shared",
    )(hp, wg, wu, wd)


def _combine_kernel(dest_ref, y_hbm, wts_ref, sh_ref, x_ref, mod_ref, nmod_ref, lng_ref, lnb_ref,
                    *rest, last):
    if last:
        xo_ref, ybuf, sem = rest
    else:
        xo_ref, ho_ref, ybuf, sem = rest
    tt = x_ref.shape[0]

    def row(tok, k):
        return pltpu.make_async_copy(y_hbm.at[pl.ds(dest_ref[k * tt + tok], 1)],
                                     ybuf.at[k, pl.ds(tok, 1)], sem)

    def start_rows(tok, carry):
        for k in range(TOP_K):
            row(tok, k).start(priority=k % 2)
        return carry

    def wait_rows(tok, carry):
        for k in range(TOP_K):
            row(tok, k).wait()
        return carry

    lax.fori_loop(0, tt, start_rows, 0)
    lax.fori_loop(0, tt, wait_rows, 0)

    routed = ybuf[0] * wts_ref[:, 0:1]
    for k in range(1, TOP_K):
        routed = routed + ybuf[k] * wts_ref[:, k:k + 1]
    xn = _resid_ln(routed + sh_ref[...], x_ref[...], mod_ref, lng_ref, lnb_ref)
    xo_ref[...] = xn
    if not last:
        ho_ref[...] = (xn * (1.0 + nmod_ref[1:2, :]) + nmod_ref[0:1, :]).astype(ho_ref.dtype)


def _combine(ys, dest_tiles, wts, shared, x2, mods, sub, ln_g, ln_b, seq, last):
    t, d = x2.shape
    tt = COMB_TT
    per_b = seq // tt
    nsub = sub if last else sub + 1
    row = lambda i: (i, 0)
    const = lambda i: (0, 0)
    outs = pl.pallas_call(
        functools.partial(_combine_kernel, last=last),
        grid=(t // tt,),
        in_specs=[
            pl.BlockSpec((tt * TOP_K,), lambda i: (i,), memory_space=pltpu.SMEM),
            pl.BlockSpec(memory_space=pl.ANY),
            pl.BlockSpec((tt, TOP_K), row),
            pl.BlockSpec((tt, d), row),
            pl.BlockSpec((tt, d), row),
            pl.BlockSpec((None, None, 3, d), lambda i: (sub, i // per_b, 0, 0)),
            pl.BlockSpec((None, None, 3, d), lambda i: (nsub, i // per_b, 0, 0)),
            pl.BlockSpec((1, d), const),
            pl.BlockSpec((1, d), const),
        ],
        out_specs=[pl.BlockSpec((tt, d), row)] * (1 if last else 2),
        out_shape=[jax.ShapeDtypeStruct((t, d), F32)] + ([] if last else [jax.ShapeDtypeStruct((t, d), BF16)]),
        scratch_shapes=[pltpu.VMEM((TOP_K, tt, d), F32), pltpu.SemaphoreType.DMA(())],
        compiler_params=_cp(("arbitrary",)),
        name="moe_combine",
    )(dest_tiles, ys, wts, shared, x2, mods, mods, ln_g.reshape(1, d), ln_b.reshape(1, d))
    return (outs[0], None) if last else (outs[0], outs[1])


def _tile_major(a, tile):
    k, t = a.shape
    return a.reshape(k, t // tile, tile).transpose(1, 0, 2).reshape(-1)


def _moe(hp, logits_t, x2, mods, sub, p, layer, seq, last):
    t = hp.shape[0]
    bm = MOE_BM
    eidx, rank, w_t, cnt = _route(logits_t, p["router_bias"])
    counts = cnt[:, 0].astype(jnp.int32)
    padded = (counts + bm - 1) // bm * bm
    pends = jnp.cumsum(padded)
    pstart = pends - padded
    nblk = (t * TOP_K) // bm + N_EXPERTS
    blk_start = jnp.arange(nblk, dtype=jnp.int32) * bm
    blk_e = jnp.minimum(jnp.sum(blk_start[:, None] >= pends[None, :], axis=1), N_EXPERTS - 1).astype(jnp.int32)
    blk_v = (blk_start < pends[-1]).astype(jnp.int32)
    has = (counts > 0).astype(jnp.int32)

    dest = _dest_rows(pstart.astype(jnp.int32), eidx, rank)
    xs = _dispatch(hp, _tile_major(dest, DISP_TD), pends.astype(jnp.int32), has, nblk * bm, DISP_TD)
    ys = _experts(xs, blk_e, blk_v, p["w_gate"], p["w_up"], p["w_down"], layer)
    sh = _shared(hp, p["ws_gate"].astype(BF16), p["ws_up"].astype(BF16), p["ws_down"].astype(BF16))
    return _combine(ys, _tile_major(dest, COMB_TT), jnp.transpose(w_t), sh, x2, mods, sub,
                    p["ln_g"], p["ln_b"], seq, last)


def kernel(x, c, ada_w, ada_b, ln_g, ln_b, gm_w_in, gm_b_in, gm_v_ln_g, gm_v_ln_b, gm_w_s, gm_b_s,
           gm_w_out, gm_b_out, at_w_qkv, at_b_qkv, at_sinks, at_w_o, at_b_o, moe_w_router,
           moe_router_bias, moe_w_gate, moe_w_up, moe_w_down, moe_ws_gate, moe_ws_up, moe_ws_down):
    bsz, seq, d = x.shape
    t = bsz * seq
    depth = ada_w.shape[0]
    x2 = x.reshape(t, d)

    rows = 8
    c_pad = jnp.zeros((rows, d), F32).at[:bsz].set(c)
    mods = _adamod(c_pad, ada_w.reshape(depth * 2, d, 3 * d), ada_b.reshape(depth * 2, 3 * d))
    mods = mods.reshape(depth * 2, rows, 3, d)

    h = _modulate(x2, mods, 0, seq)
    for i in range(depth):
        j = i // 2
        wrt = jnp.transpose(moe_w_router[i])
        if i % 2 == 0:
            z = _mm(h, gm_w_in[j].astype(BF16), gm_b_in[j], act="gelu", tm=512, tn=1024, out_dtype=BF16)
            a = _gmlp_gate(z, gm_v_ln_g[j], gm_v_ln_b[j], gm_w_s[j], gm_b_s[j])
            x2, hp, logits_t = _proj_resid_ln(a, gm_w_out[j].astype(BF16), gm_b_out[j], x2, mods, 2 * i,
                                              ln_g[i, 0], ln_b[i, 0], wrt, seq)
        else:
            qkv = _mm(h, at_w_qkv[j].astype(BF16), at_b_qkv[j], act=None, tm=512, tn=1280, out_dtype=BF16)
            o = _attention(qkv, at_sinks[j], seq)
            x2, hp, logits_t = _proj_resid_ln(o, at_w_o[j].astype(BF16), at_b_o[j], x2, mods, 2 * i,
                                              ln_g[i, 0], ln_b[i, 0], wrt, seq)
        p = dict(router_bias=moe_router_bias[i], w_gate=moe_w_gate, w_up=moe_w_up, w_down=moe_w_down,
                 ws_gate=moe_ws_gate[i], ws_up=moe_ws_up[i], ws_down=moe_ws_down[i],
                 ln_g=ln_g[i, 1], ln_b=ln_b[i, 1])
        x2, h = _moe(hp, logits_t, x2, mods, 2 * i + 1, p, i, seq, last=(i == depth - 1))
    return x2.reshape(bsz, seq, d)
```

```python
import functools
import math

import numpy as np
import jax
import jax.numpy as jnp
from jax import lax
from jax.experimental import pallas as pl
from jax.experimental.pallas import tpu as pltpu

F32 = jnp.float32
BF16 = jnp.bfloat16

DEPTH = 4
CHUNK = 128
GMLP_GROUPS = 8
HEAD_DIM = 64
Q_PER_KV = 8
ATT_BLOCK = 128
N_EXPERTS = 64
TOP_K = 8
N_EXPERT_GROUPS = 8
TOPK_GROUPS = 4
ROUTED_SCALE = 2.5
DEEPNORM_ALPHA = (2.0 * DEPTH) ** 0.25
LN_EPS = 1e-5
NEG_INF = -1e30

LANES = 128
VMEM_LIMIT = 56 * 1024 * 1024

MOE_BM = 256
COMB_TT = 128
DISP_TD = 256


def _cp(sem, vmem=VMEM_LIMIT):
    return pltpu.CompilerParams(dimension_semantics=sem, vmem_limit_bytes=vmem)


def _layer_norm(x, g, b):
    mu = jnp.mean(x, axis=-1, keepdims=True)
    xc = x - mu
    var = jnp.mean(xc * xc, axis=-1, keepdims=True)
    return xc * lax.rsqrt(var + LN_EPS) * g + b


def _alibi_slopes(n):
    p = 2 ** int(math.floor(math.log2(n)))
    base = [2.0 ** (-8.0 * (i + 1) / p) for i in range(p)]
    extra = [2.0 ** (-4.0 * (2 * i + 1) / p) for i in range(n - p)]
    return [float(np.float32(s)) for s in base + extra]


def _adamod_kernel(c_ref, w_ref, b_ref, o_ref):
    c = c_ref[...]
    ca = (c * jax.nn.sigmoid(c)).astype(BF16)
    o_ref[...] = jnp.dot(ca, w_ref[...].astype(BF16), preferred_element_type=F32) + b_ref[...]


def _adamod(c_pad, ada_w, ada_b):
    ns, d, d3 = ada_w.shape
    rows = c_pad.shape[0]
    tn = 768
    return pl.pallas_call(
        _adamod_kernel,
        grid=(ns, d3 // tn),
        in_specs=[
            pl.BlockSpec((rows, d), lambda s, j: (0, 0)),
            pl.BlockSpec((None, d, tn), lambda s, j: (s, 0, j)),
            pl.BlockSpec((None, 1, tn), lambda s, j: (s, 0, j)),
        ],
        out_specs=pl.BlockSpec((None, rows, tn), lambda s, j: (s, 0, j)),
        out_shape=jax.ShapeDtypeStruct((ns, rows, d3), F32),
        compiler_params=_cp(("arbitrary", "arbitrary")),
        name="adamod",
    )(c_pad, ada_w, ada_b.reshape(ns, 1, d3))


def _modulate_kernel(x_ref, mod_ref, h_ref):
    h = x_ref[...] * (1.0 + mod_ref[1:2, :]) + mod_ref[0:1, :]
    h_ref[...] = h.astype(h_ref.dtype)


def _modulate(x2, mods, sub, seq):
    t, d = x2.shape
    tm = 512
    per_b = seq // tm
    return pl.pallas_call(
        _modulate_kernel,
        grid=(t // tm,),
        in_specs=[
            pl.BlockSpec((tm, d), lambda i: (i, 0)),
            pl.BlockSpec((None, None, 3, d), lambda i: (sub, i // per_b, 0, 0)),
        ],
        out_specs=pl.BlockSpec((tm, d), lambda i: (i, 0)),
        out_shape=jax.ShapeDtypeStruct((t, d), BF16),
        compiler_params=_cp(("arbitrary",)),
        name="modulate",
    )(x2, mods)


def _mm_kernel(a_ref, w_ref, b_ref, o_ref, *, act):
    acc = jnp.dot(a_ref[...], w_ref[...], preferred_element_type=F32) + b_ref[...]
    if act == "gelu":
        acc = 0.5 * acc * (1.0 + lax.erf(acc * np.float32(math.sqrt(0.5))))
    o_ref[...] = acc.astype(o_ref.dtype)


def _mm(a, w, b, *, act, tm, tn, out_dtype):
    m, k = a.shape
    n = w.shape[1]
    return pl.pallas_call(
        functools.partial(_mm_kernel, act=act),
        grid=(m // tm, n // tn),
        in_specs=[
            pl.BlockSpec((tm, k), lambda i, j: (i, 0)),
            pl.BlockSpec((k, tn), lambda i, j: (0, j)),
            pl.BlockSpec((1, tn), lambda i, j: (0, j)),
        ],
        out_specs=pl.BlockSpec((tm, tn), lambda i, j: (i, j)),
        out_shape=jax.ShapeDtypeStruct((m, n), out_dtype),
        compiler_params=_cp(("arbitrary", "arbitrary")),
        name="mm_" + (act or "bias"),
    )(a, w, b.reshape(1, n))


def _gate_kernel(u_ref, v_ref, g_ref, b_ref, ws_ref, bst_ref, o_ref, *, nchunk, gw):
    vn = _layer_norm(v_ref[...].astype(F32), g_ref[...], b_ref[...]).astype(BF16)
    row = lax.broadcasted_iota(jnp.int32, (CHUNK, CHUNK), 0)
    col = lax.broadcasted_iota(jnp.int32, (CHUNK, CHUNK), 1)
    causal = row >= col
    for g in range(GMLP_GROUPS):
        wsg = jnp.where(causal, ws_ref[g], 0.0).astype(BF16)
        bs = bst_ref[:, g:g + 1]
        for c in range(nchunk):
            rs = slice(c * CHUNK, (c + 1) * CHUNK)
            cs = slice(g * gw, (g + 1) * gw)
            s = jnp.dot(wsg, vn[rs, cs], preferred_element_type=F32) + bs
            o_ref[rs, cs] = (u_ref[rs, cs].astype(F32) * s).astype(o_ref.dtype)


def _gmlp_gate(z, ln_g, ln_b, w_s, b_s):
    t, n2 = z.shape
    half = n2 // 2
    tm = 2 * CHUNK
    gw = half // GMLP_GROUPS
    return pl.pallas_call(
        functools.partial(_gate_kernel, nchunk=tm // CHUNK, gw=gw),
        grid=(t // tm,),
        in_specs=[
            pl.BlockSpec((tm, half), lambda i: (i, 0)),
            pl.BlockSpec((tm, half), lambda i: (i, 1)),
            pl.BlockSpec((1, half), lambda i: (0, 0)),
            pl.BlockSpec((1, half), lambda i: (0, 0)),
            pl.BlockSpec((GMLP_GROUPS, CHUNK, CHUNK), lambda i: (0, 0, 0)),
            pl.BlockSpec((CHUNK, GMLP_GROUPS), lambda i: (0, 0)),
        ],
        out_specs=pl.BlockSpec((tm, half), lambda i: (i, 0)),
        out_shape=jax.ShapeDtypeStruct((t, half), BF16),
        compiler_params=_cp(("arbitrary",)),
        name="gmlp_gate",
    )(z, z, ln_g.reshape(1, half), ln_b.reshape(1, half), w_s, jnp.transpose(b_s))


def _attn_kernel(sink_ref, q_ref, kc_ref, vc_ref, kp_ref, vp_ref, o_ref, bias_ref, p_ref,
                 *, nb, n_kv, slopes):
    blk = ATT_BLOCK
    pairs = Q_PER_KV // 2
    i = pl.program_id(0)

    @pl.when(i == 0)
    def _build_bias():
        qi = lax.broadcasted_iota(jnp.int32, (blk, 2 * blk), 0)
        kj = lax.broadcasted_iota(jnp.int32, (blk, 2 * blk), 1)
        dist = qi + blk - kj
        valid = (dist >= 0) & (dist < blk)
        distf = dist.astype(F32)
        for g in range(n_kv):
            for p in range(pairs):
                for j in range(2):
                    slope = slopes[g * Q_PER_KV + 2 * p + j]
                    base = jnp.where(valid, -slope * distf, NEG_INF)
                    rs = slice(p * blk, (p + 1) * blk)
                    cs = slice(j * 2 * blk, (j + 1) * 2 * blk)
                    bias_ref[0, g, rs, cs] = base
                    bias_ref[1, g, rs, cs] = jnp.where(kj < blk, NEG_INF, base)

    first = (i % nb == 0).astype(jnp.int32)
    lane = lax.broadcasted_iota(jnp.int32, (2 * blk, LANES), 1)
    lo = lane < HEAD_DIM
    lo_q = lax.broadcasted_iota(jnp.int32, (blk, LANES), 1) < HEAD_DIM

    def blockdiag(prev_ref, cur_ref, g, scale):
        c, half = g // 2, g % 2
        cs = slice(c * LANES, (c + 1) * LANES)
        t2 = jnp.concatenate([prev_ref[:, cs], cur_ref[:, cs]], axis=0).astype(F32) * scale
        rolled = pltpu.roll(t2, HEAD_DIM, 1)
        if half == 0:
            a = jnp.where(lo, t2, 0.0)
            b = jnp.where(lo, 0.0, rolled)
        else:
            a = jnp.where(lo, rolled, 0.0)
            b = jnp.where(lo, 0.0, t2)
        return jnp.concatenate([a, b], axis=0).astype(BF16)

    for g in range(n_kv):
        kbd = blockdiag(kp_ref, kc_ref, g, HEAD_DIM ** -0.5)
        vbd = blockdiag(vp_ref, vc_ref, g, 1.0)
        qg = jnp.concatenate(
            [q_ref[:, (g * pairs + p) * LANES:(g * pairs + p + 1) * LANES] for p in range(pairs)], axis=0)
        s = lax.dot_general(qg, kbd, (((1,), (1,)), ((), ())), preferred_element_type=F32)
        s = s + bias_ref[first, g]
        inv = []
        for p in range(pairs):
            for j in range(2):
                rs = slice(p * blk, (p + 1) * blk)
                cs = slice(j * 2 * blk, (j + 1) * 2 * blk)
                sub = s[rs, cs]
                sink = sink_ref[g * Q_PER_KV + 2 * p + j]
                m = jnp.maximum(jnp.max(sub, axis=-1, keepdims=True), sink)
                e = jnp.exp(sub - m)
                denom = jnp.sum(e, axis=-1, keepdims=True) + jnp.exp(sink - m)
                p_ref[rs, cs] = e.astype(BF16)
                inv.append(1.0 / denom)
        o = jnp.dot(p_ref[...], vbd, preferred_element_type=F32)
        for p in range(pairs):
            scale = jnp.where(lo_q, inv[2 * p], inv[2 * p + 1])
            col = (g * pairs + p) * LANES
            o_ref[:, col:col + LANES] = (o[p * blk:(p + 1) * blk] * scale).astype(o_ref.dtype)


def _attention(qkv, sinks, seq):
    t, w = qkv.shape
    blk = ATT_BLOCK
    nb = seq // blk
    n_kv = (w // HEAD_DIM) // (Q_PER_KV + 2)
    dq = n_kv * Q_PER_KV * HEAD_DIM
    dkv = n_kv * HEAD_DIM
    kcol = dq // dkv
    pairs = Q_PER_KV // 2
    slopes = _alibi_slopes(n_kv * Q_PER_KV)

    def prev(i):
        return jnp.where(i % nb == 0, i, i - 1)

    return pl.pallas_call(
        functools.partial(_attn_kernel, nb=nb, n_kv=n_kv, slopes=slopes),
        grid=(t // blk,),
        in_specs=[
            pl.BlockSpec(memory_space=pltpu.SMEM),
            pl.BlockSpec((blk, dq), lambda i: (i, 0)),
            pl.BlockSpec((blk, dkv), lambda i: (i, kcol)),
            pl.BlockSpec((blk, dkv), lambda i: (i, kcol + 1)),
            pl.BlockSpec((blk, dkv), lambda i: (prev(i), kcol)),
            pl.BlockSpec((blk, dkv), lambda i: (prev(i), kcol + 1)),
        ],
        out_specs=pl.BlockSpec((blk, dq), lambda i: (i, 0)),
        out_shape=jax.ShapeDtypeStruct((t, dq), BF16),
        scratch_shapes=[
            pltpu.VMEM((2, n_kv, pairs * blk, 4 * blk), F32),
            pltpu.VMEM((pairs * blk, 4 * blk), BF16),
        ],
        compiler_params=_cp(("arbitrary",)),
        name="swa_attention",
    )(sinks, qkv, qkv, qkv, qkv, qkv)


SUBLANES = 8


def _rows_from_tiles(x3):
    xt = jnp.transpose(x3, (1, 0, 2))
    return jnp.concatenate([xt[s] for s in range(x3.shape[1])], axis=1)


def _tiles_from_rows(x2):
    xt = jnp.stack([x2[:, s * LANES:(s + 1) * LANES] for s in range(x2.shape[1] // LANES)], axis=0)
    return jnp.transpose(xt, (1, 0, 2))


def _pack_rows(h):
    half = h.shape[1] // 2
    bits = lax.bitcast_convert_type(h.astype(BF16).astype(F32), jnp.uint32)
    return (bits[:, :half] >> 16) | (bits[:, half:] & jnp.uint32(0xFFFF0000))


def _unpack_f32(p):
    lo = lax.bitcast_convert_type(p << 16, F32)
    hi = lax.bitcast_convert_type(p & jnp.uint32(0xFFFF0000), F32)
    return lo, hi


def _unpack_rows(p):
    lo, hi = _unpack_f32(p)
    return lo.astype(BF16), hi.astype(BF16)


def _swiglu_packed(p, wg, wu, wd):
    lo, hi = _unpack_rows(p)
    half = p.shape[1]
    gt = (jnp.dot(lo, wg[:half], preferred_element_type=F32)
          + jnp.dot(hi, wg[half:], preferred_element_type=F32))
    up = (jnp.dot(lo, wu[:half], preferred_element_type=F32)
          + jnp.dot(hi, wu[half:], preferred_element_type=F32))
    act = (gt * jax.nn.sigmoid(gt) * up).astype(BF16)
    return jnp.dot(act, wd, preferred_element_type=F32)


def _resid_ln(y, x, mod_ref, lng_ref, lnb_ref):
    r = DEEPNORM_ALPHA * x + (1.0 + mod_ref[2:3, :]) * y
    return _layer_norm(r, lng_ref[...], lnb_ref[...])


def _proj_kernel(a_ref, w_ref, b_ref, x_ref, mod_ref, nmod_ref, lng_ref, lnb_ref, wrt_ref,
                 xo_ref, hp_ref, lt_ref, acc_ref, *, nk, slab):
    k = pl.program_id(1)

    @pl.when(k == 0)
    def _zero():
        acc_ref[...] = jnp.zeros_like(acc_ref)

    acc_ref[...] += jnp.dot(a_ref[...], w_ref[...], preferred_element_type=F32)

    @pl.when(k == nk - 1)
    def _finish():
        for r0 in range(0, acc_ref.shape[0], slab):
            rs = slice(r0, r0 + slab)
            xn = _resid_ln(acc_ref[rs, :] + b_ref[...], x_ref[rs, :], mod_ref, lng_ref, lnb_ref)
            xo_ref[rs, :] = xn
            h = xn * (1.0 + nmod_ref[1:2, :]) + nmod_ref[0:1, :]
            hp_ref[rs] = _tiles_from_rows(_pack_rows(h))
            lt_ref[:, rs] = lax.dot_general(wrt_ref[...], h, (((1,), (1,)), ((), ())),
                                            precision=lax.Precision.HIGHEST,
                                            preferred_element_type=F32)


def _proj_resid_ln(a, w, b, x2, mods, sub, ln_g, ln_b, w_router_t, seq):
    t, kdim = a.shape
    d = w.shape[1]
    tm = 512
    tk = min(kdim, 1024)
    nk = kdim // tk
    per_b = seq // tm
    ne = w_router_t.shape[0]
    row = lambda i, k: (i, 0)
    const = lambda i, k: (0, 0)
    return pl.pallas_call(
        functools.partial(_proj_kernel, nk=nk, slab=128),
        grid=(t // tm, nk),
        in_specs=[
            pl.BlockSpec((tm, tk), lambda i, k: (i, k)),
            pl.BlockSpec((tk, d), lambda i, k: (k, 0)),
            pl.BlockSpec((1, d), const),
            pl.BlockSpec((tm, d), row),
            pl.BlockSpec((None, None, 3, d), lambda i, k: (sub, i // per_b, 0, 0)),
            pl.BlockSpec((None, None, 3, d), lambda i, k: (sub + 1, i // per_b, 0, 0)),
            pl.BlockSpec((1, d), const),
            pl.BlockSpec((1, d), const),
            pl.BlockSpec((ne, d), const),
        ],
        out_specs=[
            pl.BlockSpec((tm, d), row),
            pl.BlockSpec((tm, SUBLANES, LANES), lambda i, k: (i, 0, 0)),
            pl.BlockSpec((ne, tm), lambda i, k: (0, i)),
        ],
        out_shape=[
            jax.ShapeDtypeStruct((t, d), F32),
            jax.ShapeDtypeStruct((t, SUBLANES, LANES), jnp.uint32),
            jax.ShapeDtypeStruct((ne, t), F32),
        ],
        scratch_shapes=[pltpu.VMEM((tm, d), F32)],
        compiler_params=_cp(("arbitrary", "arbitrary")),
        name="proj_resid_ln",
    )(a, w, b.reshape(1, d), x2, mods, mods, ln_g.reshape(1, d), ln_b.reshape(1, d), w_router_t)


def _route_kernel(lg_ref, bias_ref, eidx_ref, rank_ref, w_ref, cnt_ref, tri_ref, carry_ref):
    i = pl.program_id(0)
    ne, tt = lg_ref.shape
    gsz = ne // N_EXPERT_GROUPS
    neg = -jnp.inf

    @pl.when(i == 0)
    def _init():
        r = lax.broadcasted_iota(jnp.int32, (tt, tt), 0)
        c = lax.broadcasted_iota(jnp.int32, (tt, tt), 1)
        tri_ref[...] = jnp.where(r < c, 1.0, 0.0).astype(BF16)
        carry_ref[...] = jnp.zeros_like(carry_ref)

    def amax(v, ids, limit):
        m = jnp.max(v, axis=0, keepdims=True)
        idx = jnp.min(jnp.where(v == m, ids, limit), axis=0, keepdims=True)
        return m, idx

    s = jax.nn.sigmoid(lg_ref[...])
    sel = s + bias_ref[:, 0:1]

    ids_g = lax.broadcasted_iota(jnp.int32, (gsz, tt), 0).astype(F32)
    gscore = []
    for g in range(N_EXPERT_GROUPS):
        xg = sel[g * gsz:(g + 1) * gsz]
        m1, i1 = amax(xg, ids_g, float(gsz))
        m2 = jnp.max(jnp.where(ids_g == i1, neg, xg), axis=0, keepdims=True)
        gscore.append(m1 + m2)
    work = jnp.concatenate(gscore, axis=0)
    ids_n = lax.broadcasted_iota(jnp.int32, (N_EXPERT_GROUPS, tt), 0).astype(F32)
    gsel = jnp.zeros((N_EXPERT_GROUPS, tt), F32)
    for _ in range(TOPK_GROUPS):
        _, ig = amax(work, ids_n, float(N_EXPERT_GROUPS))
        pick = ids_n == ig
        gsel = jnp.where(pick, 1.0, gsel)
        work = jnp.where(pick, neg, work)
    work = jnp.concatenate(
        [jnp.where(gsel[g:g + 1] > 0.5, sel[g * gsz:(g + 1) * gsz], NEG_INF)
         for g in range(N_EXPERT_GROUPS)], axis=0)

    ids_e = lax.broadcasted_iota(jnp.int32, (ne, tt), 0).astype(F32)
    chosen = jnp.zeros((ne, tt), F32)
    picked, svals = [], []
    for _ in range(TOP_K):
        _, ik = amax(work, ids_e, float(ne))
        pick = ids_e == ik
        chosen = jnp.where(pick, 1.0, chosen)
        work = jnp.where(pick, neg, work)
        picked.append(ik)
        svals.append(jnp.sum(jnp.where(pick, s, 0.0), axis=0, keepdims=True))
    denom = svals[0]
    for k in range(1, TOP_K):
        denom = denom + svals[k]

    prefix = jnp.dot(chosen.astype(BF16), tri_ref[...], preferred_element_type=F32)
    rank_dense = prefix + carry_ref[:, 0:1]
    carry_ref[...] = carry_ref[...] + jnp.sum(chosen, axis=1, keepdims=True)
    cnt_ref[...] = carry_ref[...]
    for k in range(TOP_K):
        pick = ids_e == picked[k]
        eidx_ref[k:k + 1, :] = picked[k].astype(jnp.int32)
        rank_ref[k:k + 1, :] = jnp.sum(jnp.where(pick, rank_dense, 0.0), axis=0,
                                       keepdims=True).astype(jnp.int32)
        w_ref[k:k + 1, :] = svals[k] / denom * ROUTED_SCALE


def _route(logits_t, router_bias):
    ne, t = logits_t.shape
    tt = 1024
    bias = jnp.broadcast_to(router_bias.astype(F32)[:, None], (ne, LANES))
    tok = lambda i: (0, i)
    const = lambda i: (0, 0)
    return pl.pallas_call(
        _route_kernel,
        grid=(t // tt,),
        in_specs=[pl.BlockSpec((ne, tt), tok), pl.BlockSpec((ne, LANES), const)],
        out_specs=[
            pl.BlockSpec((TOP_K, tt), tok),
            pl.BlockSpec((TOP_K, tt), tok),
            pl.BlockSpec((TOP_K, tt), tok),
            pl.BlockSpec((ne, LANES), const),
        ],
        out_shape=[
            jax.ShapeDtypeStruct((TOP_K, t), jnp.int32),
            jax.ShapeDtypeStruct((TOP_K, t), jnp.int32),
            jax.ShapeDtypeStruct((TOP_K, t), F32),
            jax.ShapeDtypeStruct((ne, LANES), F32),
        ],
        scratch_shapes=[pltpu.VMEM((tt, tt), BF16), pltpu.VMEM((ne, LANES), F32)],
        compiler_params=_cp(("arbitrary",)),
        name="moe_route",
    )(logits_t, bias)


def _dest_kernel(pstart_ref, eidx_ref, rank_ref, dest_ref):
    e = eidx_ref[...]
    base = jnp.zeros_like(e)
    for x in range(N_EXPERTS):
        base = jnp.where(e == x, pstart_ref[x], base)
    dest_ref[...] = rank_ref[...] + base


def _dest_rows(pstart, eidx, rank):
    return pl.pallas_call(
        _dest_kernel,
        in_specs=[pl.BlockSpec(memory_space=pltpu.SMEM),
                  pl.BlockSpec(memory_space=pltpu.VMEM),
                  pl.BlockSpec(memory_space=pltpu.VMEM)],
        out_specs=pl.BlockSpec(memory_space=pltpu.VMEM),
        out_shape=jax.ShapeDtypeStruct(eidx.shape, jnp.int32),
        name="moe_dest",
    )(pstart, eidx, rank)


def _dispatch_kernel(pend_ref, has_ref, dest_ref, hp_ref, xs_hbm, zbuf, sem, zsem):
    i = pl.program_id(0)
    td = hp_ref.shape[0]
    bm = zbuf.shape[0]

    @pl.when(i == 0)
    def _zero_tails():
        zbuf[...] = jnp.zeros_like(zbuf)

        def tail(e):
            start_row = pl.multiple_of(pend_ref[e] - bm, bm)
            return pltpu.make_async_copy(zbuf, xs_hbm.at[pl.ds(start_row, bm)], zsem)

        def start(e, carry):
            @pl.when(has_ref[e] == 1)
            def _():
                tail(e).start()
            return carry

        def wait(e, carry):
            @pl.when(has_ref[e] == 1)
            def _():
                tail(e).wait()
            return carry

        lax.fori_loop(0, N_EXPERTS, start, 0)
        lax.fori_loop(0, N_EXPERTS, wait, 0)

    def row(tok, k):
        return pltpu.make_async_copy(hp_ref.at[pl.ds(tok, 1)],
                                     xs_hbm.at[pl.ds(dest_ref[k * td + tok], 1)], sem)

    def start_rows(g, carry):
        for j in range(SUBLANES):
            for k in range(TOP_K):
                row(g * SUBLANES + j, k).start(priority=k % 2)
        return carry

    def wait_rows(g, carry):
        for j in range(SUBLANES):
            for k in range(TOP_K):
                row(g * SUBLANES + j, k).wait()
        return carry

    lax.fori_loop(0, td // SUBLANES, start_rows, 0)
    lax.fori_loop(0, td // SUBLANES, wait_rows, 0)


def _dispatch(hp, dest_tiles, pends, has, n_rows, td):
    t = hp.shape[0]
    tile = hp.shape[1:]
    grid_spec = pltpu.PrefetchScalarGridSpec(
        num_scalar_prefetch=2,
        grid=(t // td,),
        in_specs=[
            pl.BlockSpec((TOP_K * td,), lambda i, pe, ha: (i,), memory_space=pltpu.SMEM),
            pl.BlockSpec((td,) + tile, lambda i, pe, ha: (i, 0, 0)),
        ],
        out_specs=pl.BlockSpec(memory_space=pl.ANY),
        scratch_shapes=[pltpu.VMEM((MOE_BM,) + tile, jnp.uint32), pltpu.SemaphoreType.DMA(()),
                        pltpu.SemaphoreType.DMA(())],
    )
    return pl.pallas_call(
        _dispatch_kernel,
        grid_spec=grid_spec,
        out_shape=jax.ShapeDtypeStruct((n_rows,) + tile, jnp.uint32),
        compiler_params=_cp(("arbitrary",)),
        name="moe_dispatch",
    )(pends, has, dest_tiles, hp)


def _experts_kernel(blk_e_ref, blk_v_ref, xs_ref, wg_ref, wu_ref, wd_ref, o_ref, wgb, wub, wdb):
    b = pl.program_id(0)
    valid = blk_v_ref[b] == 1
    changed = (b == 0) | (blk_e_ref[b] != blk_e_ref[jnp.maximum(b - 1, 0)])

    @pl.when(valid & changed)
    def _cast_weights():
        wgb[...] = wg_ref[...].astype(BF16)
        wub[...] = wu_ref[...].astype(BF16)
        wdb[...] = wd_ref[...].astype(BF16)

    @pl.when(valid)
    def _compute():
        y = _swiglu_packed(_rows_from_tiles(xs_ref[...]), wgb[...], wub[...], wdb[...])
        o_ref[...] = _tiles_from_rows(_pack_rows(y))

    @pl.when(jnp.logical_not(valid))
    def _skip():
        o_ref[...] = jnp.zeros_like(o_ref)


def _experts(xs, blk_e, blk_v, wg, wu, wd, layer):
    _, ne, d, ff = wg.shape
    bm = MOE_BM
    nblk = blk_e.shape[0]
    tile = xs.shape[1:]
    wmap = lambda b, be, bv: (layer, be[b], 0, 0)
    grid_spec = pltpu.PrefetchScalarGridSpec(
        num_scalar_prefetch=2,
        grid=(nblk,),
        in_specs=[
            pl.BlockSpec((bm,) + tile, lambda b, be, bv: (jnp.where(bv[b] == 1, b, 0), 0, 0)),
            pl.BlockSpec((None, None, d, ff), wmap),
            pl.BlockSpec((None, None, d, ff), wmap),
            pl.BlockSpec((None, None, ff, d), wmap),
        ],
        out_specs=pl.BlockSpec((bm,) + tile, lambda b, be, bv: (b, 0, 0)),
        scratch_shapes=[pltpu.VMEM((d, ff), BF16), pltpu.VMEM((d, ff), BF16), pltpu.VMEM((ff, d), BF16)],
    )
    return pl.pallas_call(
        _experts_kernel,
        grid_spec=grid_spec,
        out_shape=jax.ShapeDtypeStruct((nblk * bm,) + tile, jnp.uint32),
        compiler_params=_cp(("arbitrary",)),
        name="moe_experts",
    )(blk_e, blk_v, xs, wg, wu, wd)


def _shared_kernel(hp_ref, wg_ref, wu_ref, wd_ref, o_ref):
    o_ref[...] = _swiglu_packed(_rows_from_tiles(hp_ref[...]), wg_ref[...], wu_ref[...], wd_ref[...])


def _shared(hp, wg, wu, wd):
    t = hp.shape[0]
    d, ff = wg.shape
    tm = 512
    const = lambda i: (0, 0)
    return pl.pallas_call(
        _shared_kernel,
        grid=(t // tm,),
        in_specs=[
            pl.BlockSpec((tm,) + hp.shape[1:], lambda i: (i, 0, 0)),
            pl.BlockSpec((d, ff), const),
            pl.BlockSpec((d, ff), const),
            pl.BlockSpec((ff, d), const),
        ],
        out_specs=pl.BlockSpec((tm, d), lambda i: (i, 0)),
        out_shape=jax.ShapeDtypeStruct((t, d), F32),
        compiler_params=_cp(("arbitrary",)),
        name="moe_shared",
    )(hp, wg, wu, wd)


def _combine_kernel(dcur_ref, dnxt_ref, y_hbm, wts_ref, sh_ref, x_ref, mod_ref, nmod_ref, lng_ref,
                    lnb_ref, *rest, last):
    if last:
        xo_ref, ybuf, sem = rest
    else:
        xo_ref, ho_ref, ybuf, sem = rest
    tt = x_ref.shape[0]
    i = pl.program_id(0)
    slot = i % 2

    def row(dref, slot_, tok, k):
        return pltpu.make_async_copy(y_hbm.at[pl.ds(dref[k * tt + tok], 1)],
                                     ybuf.at[slot_, pl.ds(k * tt + tok, 1)], sem.at[slot_])

    def start_tile(dref, slot_):
        def body(g, carry):
            for j in range(SUBLANES):
                for k in range(TOP_K):
                    row(dref, slot_, g * SUBLANES + j, k).start(priority=k % 2)
            return carry

        lax.fori_loop(0, tt // SUBLANES, body, 0)

    @pl.when(i == 0)
    def _prime():
        start_tile(dcur_ref, 0)

    @pl.when(i + 1 < pl.num_programs(0))
    def _prefetch():
        start_tile(dnxt_ref, 1 - slot)

    def wait_rows(g, carry):
        for j in range(SUBLANES):
            for k in range(TOP_K):
                row(dcur_ref, slot, g * SUBLANES + j, k).wait()
        return carry

    lax.fori_loop(0, tt // SUBLANES, wait_rows, 0)

    wb = [jnp.broadcast_to(wts_ref[:, k:k + 1], (tt, LANES)) for k in range(TOP_K)]
    lo_cols = [None] * SUBLANES
    hi_cols = [None] * SUBLANES
    for k in range(TOP_K):
        yt = jnp.transpose(ybuf[slot, pl.ds(k * tt, tt)], (1, 0, 2))
        for s in range(SUBLANES):
            lo, hi = _unpack_f32(yt[s])
            lo_cols[s] = lo * wb[k] if k == 0 else lo_cols[s] + lo * wb[k]
            hi_cols[s] = hi * wb[k] if k == 0 else hi_cols[s] + hi * wb[k]
    routed = jnp.concatenate(lo_cols + hi_cols, axis=1)
    xn = _resid_ln(routed + sh_ref[...], x_ref[...], mod_ref, lng_ref, lnb_ref)
    xo_ref[...] = xn
    if not last:
        ho_ref[...] = (xn * (1.0 + nmod_ref[1:2, :]) + nmod_ref[0:1, :]).astype(ho_ref.dtype)


def _combine(ys, dest_tiles, wts, shared, x2, mods, sub, ln_g, ln_b, seq, last):
    t, d = x2.shape
    tt = COMB_TT
    nt = t // tt
    per_b = seq // tt
    nsub = sub if last else sub + 1
    row = lambda i: (i, 0)
    const = lambda i: (0, 0)
    outs = pl.pallas_call(
        functools.partial(_combine_kernel, last=last),
        grid=(nt,),
        in_specs=[
            pl.BlockSpec((tt * TOP_K,), lambda i: (i,), memory_space=pltpu.SMEM),
            pl.BlockSpec((tt * TOP_K,), lambda i: (jnp.minimum(i + 1, nt - 1),), memory_space=pltpu.SMEM),
            pl.BlockSpec(memory_space=pl.ANY),
            pl.BlockSpec((tt, TOP_K), row),
            pl.BlockSpec((tt, d), row),
            pl.BlockSpec((tt, d), row),
            pl.BlockSpec((None, None, 3, d), lambda i: (sub, i // per_b, 0, 0)),
            pl.BlockSpec((None, None, 3, d), lambda i: (nsub, i // per_b, 0, 0)),
            pl.BlockSpec((1, d), const),
            pl.BlockSpec((1, d), const),
        ],
        out_specs=[pl.BlockSpec((tt, d), row)] * (1 if last else 2),
        out_shape=[jax.ShapeDtypeStruct((t, d), F32)] + ([] if last else [jax.ShapeDtypeStruct((t, d), BF16)]),
        scratch_shapes=[pltpu.VMEM((2, TOP_K * tt) + ys.shape[1:], jnp.uint32),
                        pltpu.SemaphoreType.DMA((2,))],
        compiler_params=_cp(("arbitrary",)),
        name="moe_combine",
    )(dest_tiles, dest_tiles, ys, wts, shared, x2, mods, mods, ln_g.reshape(1, d), ln_b.reshape(1, d))
    return (outs[0], None) if last else (outs[0], outs[1])


def _tile_major(a, tile):
    k, t = a.shape
    return a.reshape(k, t // tile, tile).transpose(1, 0, 2).reshape(-1)


def _moe(hp, logits_t, x2, mods, sub, p, layer, seq, last):
    t = hp.shape[0]
    bm = MOE_BM
    eidx, rank, w_t, cnt = _route(logits_t, p["router_bias"])
    counts = cnt[:, 0].astype(jnp.int32)
    padded = (counts + bm - 1) // bm * bm
    pends = jnp.cumsum(padded)
    pstart = pends - padded
    nblk = (t * TOP_K) // bm + N_EXPERTS
    blk_start = jnp.arange(nblk, dtype=jnp.int32) * bm
    blk_e = jnp.minimum(jnp.sum(blk_start[:, None] >= pends[None, :], axis=1), N_EXPERTS - 1).astype(jnp.int32)
    blk_v = (blk_start < pends[-1]).astype(jnp.int32)
    has = (counts > 0).astype(jnp.int32)

    dest = _dest_rows(pstart.astype(jnp.int32), eidx, rank)
    xs = _dispatch(hp, _tile_major(dest, DISP_TD), pends.astype(jnp.int32), has, nblk * bm, DISP_TD)
    ys = _experts(xs, blk_e, blk_v, p["w_gate"], p["w_up"], p["w_down"], layer)
    sh = _shared(hp, p["ws_gate"].astype(BF16), p["ws_up"].astype(BF16), p["ws_down"].astype(BF16))
    return _combine(ys, _tile_major(dest, COMB_TT), jnp.transpose(w_t), sh, x2, mods, sub,
                    p["ln_g"], p["ln_b"], seq, last)


def kernel(x, c, ada_w, ada_b, ln_g, ln_b, gm_w_in, gm_b_in, gm_v_ln_g, gm_v_ln_b, gm_w_s, gm_b_s,
           gm_w_out, gm_b_out, at_w_qkv, at_b_qkv, at_sinks, at_w_o, at_b_o, moe_w_router,
           moe_router_bias, moe_w_gate, moe_w_up, moe_w_down, moe_ws_gate, moe_ws_up, moe_ws_down):
    bsz, seq, d = x.shape
    t = bsz * seq
    depth = ada_w.shape[0]
    x2 = x.reshape(t, d)

    rows = 8
    c_pad = jnp.zeros((rows, d), F32).at[:bsz].set(c)
    mods = _adamod(c_pad, ada_w.reshape(depth * 2, d, 3 * d), ada_b.reshape(depth * 2, 3 * d))
    mods = mods.reshape(depth * 2, rows, 3, d)

    h = _modulate(x2, mods, 0, seq)
    for i in range(depth):
        j = i // 2
        wrt = jnp.transpose(moe_w_router[i])
        if i % 2 == 0:
            z = _mm(h, gm_w_in[j].astype(BF16), gm_b_in[j], act="gelu", tm=512, tn=1024, out_dtype=BF16)
            a = _gmlp_gate(z, gm_v_ln_g[j], gm_v_ln_b[j], gm_w_s[j], gm_b_s[j])
            x2, hp, logits_t = _proj_resid_ln(a, gm_w_out[j].astype(BF16), gm_b_out[j], x2, mods, 2 * i,
                                              ln_g[i, 0], ln_b[i, 0], wrt, seq)
        else:
            qkv = _mm(h, at_w_qkv[j].astype(BF16), at_b_qkv[j], act=None, tm=512, tn=1280, out_dtype=BF16)
            o = _attention(qkv, at_sinks[j], seq)
            x2, hp, logits_t = _proj_resid_ln(o, at_w_o[j].astype(BF16), at_b_o[j], x2, mods, 2 * i,
                                              ln_g[i, 0], ln_b[i, 0], wrt, seq)
        p = dict(router_bias=moe_router_bias[i], w_gate=moe_w_gate, w_up=moe_w_up, w_down=moe_w_down,
                 ws_gate=moe_ws_gate[i], ws_up=moe_ws_up[i], ws_down=moe_ws_down[i],
                 ln_g=ln_g[i, 1], ln_b=ln_b[i, 1])
        x2, h = _moe(hp, logits_t, x2, mods, 2 * i + 1, p, i, seq, last=(i == depth - 1))
    return x2.reshape(bsz, seq, d)
```

```python
import functools
import math

import numpy as np
import jax
import jax.numpy as jnp
from jax import lax
from jax.experimental import pallas as pl
from jax.experimental.pallas import tpu as pltpu

F32 = jnp.float32
BF16 = jnp.bfloat16

DEPTH = 4
CHUNK = 128
GMLP_GROUPS = 8
HEAD_DIM = 64
Q_PER_KV = 8
ATT_BLOCK = 128
N_EXPERTS = 64
TOP_K = 8
N_EXPERT_GROUPS = 8
TOPK_GROUPS = 4
ROUTED_SCALE = 2.5
DEEPNORM_ALPHA = (2.0 * DEPTH) ** 0.25
LN_EPS = 1e-5
NEG_INF = -1e30

LANES = 128
VMEM_LIMIT = 56 * 1024 * 1024

MOE_BM = 512
COMB_TT = 128
DISP_TD = 256


def _cp(sem, vmem=VMEM_LIMIT):
    return pltpu.CompilerParams(dimension_semantics=sem, vmem_limit_bytes=vmem)


def _layer_norm(x, g, b):
    mu = jnp.mean(x, axis=-1, keepdims=True)
    xc = x - mu
    var = jnp.mean(xc * xc, axis=-1, keepdims=True)
    return xc * lax.rsqrt(var + LN_EPS) * g + b


def _alibi_slopes(n):
    p = 2 ** int(math.floor(math.log2(n)))
    base = [2.0 ** (-8.0 * (i + 1) / p) for i in range(p)]
    extra = [2.0 ** (-4.0 * (2 * i + 1) / p) for i in range(n - p)]
    return [float(np.float32(s)) for s in base + extra]


def _adamod_kernel(c_ref, w_ref, b_ref, o_ref):
    c = c_ref[...]
    ca = (c * jax.nn.sigmoid(c)).astype(BF16)
    o_ref[...] = jnp.dot(ca, w_ref[...].astype(BF16), preferred_element_type=F32) + b_ref[...]


def _adamod(c_pad, ada_w, ada_b):
    ns, d, d3 = ada_w.shape
    rows = c_pad.shape[0]
    tn = 768
    return pl.pallas_call(
        _adamod_kernel,
        grid=(ns, d3 // tn),
        in_specs=[
            pl.BlockSpec((rows, d), lambda s, j: (0, 0)),
            pl.BlockSpec((None, d, tn), lambda s, j: (s, 0, j)),
            pl.BlockSpec((None, 1, tn), lambda s, j: (s, 0, j)),
        ],
        out_specs=pl.BlockSpec((None, rows, tn), lambda s, j: (s, 0, j)),
        out_shape=jax.ShapeDtypeStruct((ns, rows, d3), F32),
        compiler_params=_cp(("arbitrary", "arbitrary")),
        name="adamod",
    )(c_pad, ada_w, ada_b.reshape(ns, 1, d3))


def _modulate_kernel(x_ref, mod_ref, h_ref):
    h = x_ref[...] * (1.0 + mod_ref[1:2, :]) + mod_ref[0:1, :]
    h_ref[...] = h.astype(h_ref.dtype)


def _modulate(x2, mods, sub, seq):
    t, d = x2.shape
    tm = 512
    per_b = seq // tm
    return pl.pallas_call(
        _modulate_kernel,
        grid=(t // tm,),
        in_specs=[
            pl.BlockSpec((tm, d), lambda i: (i, 0)),
            pl.BlockSpec((None, None, 3, d), lambda i: (sub, i // per_b, 0, 0)),
        ],
        out_specs=pl.BlockSpec((tm, d), lambda i: (i, 0)),
        out_shape=jax.ShapeDtypeStruct((t, d), BF16),
        compiler_params=_cp(("arbitrary",)),
        name="modulate",
    )(x2, mods)


def _mm_kernel(a_ref, w_ref, b_ref, o_ref, *, act):
    acc = jnp.dot(a_ref[...], w_ref[...], preferred_element_type=F32) + b_ref[...]
    if act == "gelu":
        acc = 0.5 * acc * (1.0 + lax.erf(acc * np.float32(math.sqrt(0.5))))
    o_ref[...] = acc.astype(o_ref.dtype)


def _mm(a, w, b, *, act, tm, tn, out_dtype):
    m, k = a.shape
    n = w.shape[1]
    return pl.pallas_call(
        functools.partial(_mm_kernel, act=act),
        grid=(m // tm, n // tn),
        in_specs=[
            pl.BlockSpec((tm, k), lambda i, j: (i, 0)),
            pl.BlockSpec((k, tn), lambda i, j: (0, j)),
            pl.BlockSpec((1, tn), lambda i, j: (0, j)),
        ],
        out_specs=pl.BlockSpec((tm, tn), lambda i, j: (i, j)),
        out_shape=jax.ShapeDtypeStruct((m, n), out_dtype),
        compiler_params=_cp(("arbitrary", "arbitrary")),
        name="mm_" + (act or "bias"),
    )(a, w, b.reshape(1, n))


def _gate_kernel(u_ref, v_ref, g_ref, b_ref, ws_ref, bst_ref, o_ref, *, nchunk, gw):
    vn = _layer_norm(v_ref[...].astype(F32), g_ref[...], b_ref[...]).astype(BF16)
    row = lax.broadcasted_iota(jnp.int32, (CHUNK, CHUNK), 0)
    col = lax.broadcasted_iota(jnp.int32, (CHUNK, CHUNK), 1)
    causal = row >= col
    for g in range(GMLP_GROUPS):
        wsg = jnp.where(causal, ws_ref[g], 0.0).astype(BF16)
        bs = bst_ref[:, g:g + 1]
        for c in range(nchunk):
            rs = slice(c * CHUNK, (c + 1) * CHUNK)
            cs = slice(g * gw, (g + 1) * gw)
            s = jnp.dot(wsg, vn[rs, cs], preferred_element_type=F32) + bs
            o_ref[rs, cs] = (u_ref[rs, cs].astype(F32) * s).astype(o_ref.dtype)


def _gmlp_gate(z, ln_g, ln_b, w_s, b_s):
    t, n2 = z.shape
    half = n2 // 2
    tm = 2 * CHUNK
    gw = half // GMLP_GROUPS
    return pl.pallas_call(
        functools.partial(_gate_kernel, nchunk=tm // CHUNK, gw=gw),
        grid=(t // tm,),
        in_specs=[
            pl.BlockSpec((tm, half), lambda i: (i, 0)),
            pl.BlockSpec((tm, half), lambda i: (i, 1)),
            pl.BlockSpec((1, half), lambda i: (0, 0)),
            pl.BlockSpec((1, half), lambda i: (0, 0)),
            pl.BlockSpec((GMLP_GROUPS, CHUNK, CHUNK), lambda i: (0, 0, 0)),
            pl.BlockSpec((CHUNK, GMLP_GROUPS), lambda i: (0, 0)),
        ],
        out_specs=pl.BlockSpec((tm, half), lambda i: (i, 0)),
        out_shape=jax.ShapeDtypeStruct((t, half), BF16),
        compiler_params=_cp(("arbitrary",)),
        name="gmlp_gate",
    )(z, z, ln_g.reshape(1, half), ln_b.reshape(1, half), w_s, jnp.transpose(b_s))


def _attn_kernel(sink_ref, q_ref, kc_ref, vc_ref, kp_ref, vp_ref, o_ref, bias_ref, p_ref,
                 *, nb, n_kv, slopes):
    blk = ATT_BLOCK
    pairs = Q_PER_KV // 2
    i = pl.program_id(0)

    @pl.when(i == 0)
    def _build_bias():
        qi = lax.broadcasted_iota(jnp.int32, (blk, 2 * blk), 0)
        kj = lax.broadcasted_iota(jnp.int32, (blk, 2 * blk), 1)
        dist = qi + blk - kj
        valid = (dist >= 0) & (dist < blk)
        distf = dist.astype(F32)
        for g in range(n_kv):
            for p in range(pairs):
                for j in range(2):
                    slope = slopes[g * Q_PER_KV + 2 * p + j]
                    base = jnp.where(valid, -slope * distf, NEG_INF)
                    rs = slice(p * blk, (p + 1) * blk)
                    cs = slice(j * 2 * blk, (j + 1) * 2 * blk)
                    bias_ref[0, g, rs, cs] = base
                    bias_ref[1, g, rs, cs] = jnp.where(kj < blk, NEG_INF, base)

    first = (i % nb == 0).astype(jnp.int32)
    lane = lax.broadcasted_iota(jnp.int32, (2 * blk, LANES), 1)
    lo = lane < HEAD_DIM
    lo_q = lax.broadcasted_iota(jnp.int32, (blk, LANES), 1) < HEAD_DIM

    def blockdiag(prev_ref, cur_ref, g, scale):
        c, half = g // 2, g % 2
        cs = slice(c * LANES, (c + 1) * LANES)
        t2 = jnp.concatenate([prev_ref[:, cs], cur_ref[:, cs]], axis=0).astype(F32) * scale
        rolled = pltpu.roll(t2, HEAD_DIM, 1)
        if half == 0:
            a = jnp.where(lo, t2, 0.0)
            b = jnp.where(lo, 0.0, rolled)
        else:
            a = jnp.where(lo, rolled, 0.0)
            b = jnp.where(lo, 0.0, t2)
        return jnp.concatenate([a, b], axis=0).astype(BF16)

    for g in range(n_kv):
        kbd = blockdiag(kp_ref, kc_ref, g, HEAD_DIM ** -0.5)
        vbd = blockdiag(vp_ref, vc_ref, g, 1.0)
        qg = jnp.concatenate(
            [q_ref[:, (g * pairs + p) * LANES:(g * pairs + p + 1) * LANES] for p in range(pairs)], axis=0)
        s = lax.dot_general(qg, kbd, (((1,), (1,)), ((), ())), preferred_element_type=F32)
        s = s + bias_ref[first, g]
        inv = []
        for p in range(pairs):
            for j in range(2):
                rs = slice(p * blk, (p + 1) * blk)
                cs = slice(j * 2 * blk, (j + 1) * 2 * blk)
                sub = s[rs, cs]
                sink = sink_ref[g * Q_PER_KV + 2 * p + j]
                m = jnp.maximum(jnp.max(sub, axis=-1, keepdims=True), sink)
                e = jnp.exp(sub - m)
                denom = jnp.sum(e, axis=-1, keepdims=True) + jnp.exp(sink - m)
                p_ref[rs, cs] = e.astype(BF16)
                inv.append(1.0 / denom)
        o = jnp.dot(p_ref[...], vbd, preferred_element_type=F32)
        for p in range(pairs):
            scale = jnp.where(lo_q, inv[2 * p], inv[2 * p + 1])
            col = (g * pairs + p) * LANES
            o_ref[:, col:col + LANES] = (o[p * blk:(p + 1) * blk] * scale).astype(o_ref.dtype)


def _attention(qkv, sinks, seq):
    t, w = qkv.shape
    blk = ATT_BLOCK
    nb = seq // blk
    n_kv = (w // HEAD_DIM) // (Q_PER_KV + 2)
    dq = n_kv * Q_PER_KV * HEAD_DIM
    dkv = n_kv * HEAD_DIM
    kcol = dq // dkv
    pairs = Q_PER_KV // 2
    slopes = _alibi_slopes(n_kv * Q_PER_KV)

    def prev(i):
        return jnp.where(i % nb == 0, i, i - 1)

    return pl.pallas_call(
        functools.partial(_attn_kernel, nb=nb, n_kv=n_kv, slopes=slopes),
        grid=(t // blk,),
        in_specs=[
            pl.BlockSpec(memory_space=pltpu.SMEM),
            pl.BlockSpec((blk, dq), lambda i: (i, 0)),
            pl.BlockSpec((blk, dkv), lambda i: (i, kcol)),
            pl.BlockSpec((blk, dkv), lambda i: (i, kcol + 1)),
            pl.BlockSpec((blk, dkv), lambda i: (prev(i), kcol)),
            pl.BlockSpec((blk, dkv), lambda i: (prev(i), kcol + 1)),
        ],
        out_specs=pl.BlockSpec((blk, dq), lambda i: (i, 0)),
        out_shape=jax.ShapeDtypeStruct((t, dq), BF16),
        scratch_shapes=[
            pltpu.VMEM((2, n_kv, pairs * blk, 4 * blk), F32),
            pltpu.VMEM((pairs * blk, 4 * blk), BF16),
        ],
        compiler_params=_cp(("arbitrary",)),
        name="swa_attention",
    )(sinks, qkv, qkv, qkv, qkv, qkv)


SUBLANES = 8


def _rows_from_tiles(x3):
    xt = jnp.transpose(x3, (1, 0, 2))
    return jnp.concatenate([xt[s] for s in range(x3.shape[1])], axis=1)


def _tiles_from_rows(x2):
    xt = jnp.stack([x2[:, s * LANES:(s + 1) * LANES] for s in range(x2.shape[1] // LANES)], axis=0)
    return jnp.transpose(xt, (1, 0, 2))


def _pack_rows(h):
    half = h.shape[1] // 2
    bits = lax.bitcast_convert_type(h.astype(BF16).astype(F32), jnp.uint32)
    return (bits[:, :half] >> 16) | (bits[:, half:] & jnp.uint32(0xFFFF0000))


def _unpack_f32(p):
    lo = lax.bitcast_convert_type(p << 16, F32)
    hi = lax.bitcast_convert_type(p & jnp.uint32(0xFFFF0000), F32)
    return lo, hi


def _unpack_rows(p):
    lo, hi = _unpack_f32(p)
    return lo.astype(BF16), hi.astype(BF16)


def _swiglu_packed(p, wg, wu, wd):
    lo, hi = _unpack_rows(p)
    half = p.shape[1]
    gt = (jnp.dot(lo, wg[:half], preferred_element_type=F32)
          + jnp.dot(hi, wg[half:], preferred_element_type=F32))
    up = (jnp.dot(lo, wu[:half], preferred_element_type=F32)
          + jnp.dot(hi, wu[half:], preferred_element_type=F32))
    act = (gt * jax.nn.sigmoid(gt) * up).astype(BF16)
    return jnp.dot(act, wd, preferred_element_type=F32)


def _resid_ln(y, x, mod_ref, lng_ref, lnb_ref):
    r = DEEPNORM_ALPHA * x + (1.0 + mod_ref[2:3, :]) * y
    return _layer_norm(r, lng_ref[...], lnb_ref[...])


def _proj_kernel(a_ref, w_ref, b_ref, x_ref, mod_ref, nmod_ref, lng_ref, lnb_ref, wrh_ref, wrl_ref,
                 xo_ref, hp_ref, lt_ref, acc_ref, *, nk, slab):
    k = pl.program_id(1)

    @pl.when(k == 0)
    def _zero():
        acc_ref[...] = jnp.zeros_like(acc_ref)

    acc_ref[...] += jnp.dot(a_ref[...], w_ref[...], preferred_element_type=F32)

    @pl.when(k == nk - 1)
    def _finish():
        for r0 in range(0, acc_ref.shape[0], slab):
            rs = slice(r0, r0 + slab)
            xn = _resid_ln(acc_ref[rs, :] + b_ref[...], x_ref[rs, :], mod_ref, lng_ref, lnb_ref)
            xo_ref[rs, :] = xn
            h = xn * (1.0 + nmod_ref[1:2, :]) + nmod_ref[0:1, :]
            hp_ref[rs] = _tiles_from_rows(_pack_rows(h))
            h_hi = h.astype(BF16)
            h_lo = (h - h_hi.astype(F32)).astype(BF16)
            lg = (jnp.dot(h_hi, wrh_ref[...], preferred_element_type=F32)
                  + (jnp.dot(h_hi, wrl_ref[...], preferred_element_type=F32)
                     + jnp.dot(h_lo, wrh_ref[...], preferred_element_type=F32)))
            lt_ref[:, rs] = jnp.transpose(lg)[:lt_ref.shape[0]]


def _proj_resid_ln(a, w, b, x2, mods, sub, ln_g, ln_b, w_router, seq):
    t, kdim = a.shape
    d = w.shape[1]
    tm = 512
    tk = min(kdim, 1024)
    nk = kdim // tk
    per_b = seq // tm
    ne = w_router.shape[1]
    wr = jnp.zeros((d, LANES), F32).at[:, :ne].set(w_router)
    wr_hi = wr.astype(BF16)
    wr_lo = (wr - wr_hi.astype(F32)).astype(BF16)
    row = lambda i, k: (i, 0)
    const = lambda i, k: (0, 0)
    return pl.pallas_call(
        functools.partial(_proj_kernel, nk=nk, slab=128),
        grid=(t // tm, nk),
        in_specs=[
            pl.BlockSpec((tm, tk), lambda i, k: (i, k)),
            pl.BlockSpec((tk, d), lambda i, k: (k, 0)),
            pl.BlockSpec((1, d), const),
            pl.BlockSpec((tm, d), row),
            pl.BlockSpec((None, None, 3, d), lambda i, k: (sub, i // per_b, 0, 0)),
            pl.BlockSpec((None, None, 3, d), lambda i, k: (sub + 1, i // per_b, 0, 0)),
            pl.BlockSpec((1, d), const),
            pl.BlockSpec((1, d), const),
            pl.BlockSpec((d, LANES), const),
            pl.BlockSpec((d, LANES), const),
        ],
        out_specs=[
            pl.BlockSpec((tm, d), row),
            pl.BlockSpec((tm, SUBLANES, LANES), lambda i, k: (i, 0, 0)),
            pl.BlockSpec((ne, tm), lambda i, k: (0, i)),
        ],
        out_shape=[
            jax.ShapeDtypeStruct((t, d), F32),
            jax.ShapeDtypeStruct((t, SUBLANES, LANES), jnp.uint32),
            jax.ShapeDtypeStruct((ne, t), F32),
        ],
        scratch_shapes=[pltpu.VMEM((tm, d), F32)],
        compiler_params=_cp(("arbitrary", "arbitrary")),
        name="proj_resid_ln",
    )(a, w, b.reshape(1, d), x2, mods, mods, ln_g.reshape(1, d), ln_b.reshape(1, d), wr_hi, wr_lo)


def _route_kernel(lg_ref, bias_ref, eidx_ref, rank_ref, w_ref, cnt_ref, tri_ref, carry_ref):
    i = pl.program_id(0)
    ne, tt = lg_ref.shape
    gsz = ne // N_EXPERT_GROUPS
    neg = -jnp.inf

    @pl.when(i == 0)
    def _init():
        r = lax.broadcasted_iota(jnp.int32, (tt, tt), 0)
        c = lax.broadcasted_iota(jnp.int32, (tt, tt), 1)
        tri_ref[...] = jnp.where(r < c, 1.0, 0.0).astype(BF16)
        carry_ref[...] = jnp.zeros_like(carry_ref)

    def amax(v, ids, limit):
        m = jnp.max(v, axis=0, keepdims=True)
        idx = jnp.min(jnp.where(v == m, ids, limit), axis=0, keepdims=True)
        return m, idx

    s = jax.nn.sigmoid(lg_ref[...])
    sel = s + bias_ref[:, 0:1]

    ids_g = lax.broadcasted_iota(jnp.int32, (gsz, tt), 0).astype(F32)
    gscore = []
    for g in range(N_EXPERT_GROUPS):
        xg = sel[g * gsz:(g + 1) * gsz]
        m1, i1 = amax(xg, ids_g, float(gsz))
        m2 = jnp.max(jnp.where(ids_g == i1, neg, xg), axis=0, keepdims=True)
        gscore.append(m1 + m2)
    work = jnp.concatenate(gscore, axis=0)
    ids_n = lax.broadcasted_iota(jnp.int32, (N_EXPERT_GROUPS, tt), 0).astype(F32)
    gsel = jnp.zeros((N_EXPERT_GROUPS, tt), F32)
    for _ in range(TOPK_GROUPS):
        _, ig = amax(work, ids_n, float(N_EXPERT_GROUPS))
        pick = ids_n == ig
        gsel = jnp.where(pick, 1.0, gsel)
        work = jnp.where(pick, neg, work)
    work = jnp.concatenate(
        [jnp.where(gsel[g:g + 1] > 0.5, sel[g * gsz:(g + 1) * gsz], NEG_INF)
         for g in range(N_EXPERT_GROUPS)], axis=0)

    ids_e = lax.broadcasted_iota(jnp.int32, (ne, tt), 0).astype(F32)
    chosen = jnp.zeros((ne, tt), F32)
    picked, svals = [], []
    for _ in range(TOP_K):
        _, ik = amax(work, ids_e, float(ne))
        pick = ids_e == ik
        chosen = jnp.where(pick, 1.0, chosen)
        work = jnp.where(pick, neg, work)
        picked.append(ik)
        svals.append(jnp.sum(jnp.where(pick, s, 0.0), axis=0, keepdims=True))
    denom = svals[0]
    for k in range(1, TOP_K):
        denom = denom + svals[k]

    prefix = jnp.dot(chosen.astype(BF16), tri_ref[...], preferred_element_type=F32)
    rank_dense = prefix + carry_ref[:, 0:1]
    carry_ref[...] = carry_ref[...] + jnp.sum(chosen, axis=1, keepdims=True)
    cnt_ref[...] = carry_ref[...]
    for k in range(TOP_K):
        pick = ids_e == picked[k]
        eidx_ref[k:k + 1, :] = picked[k].astype(jnp.int32)
        rank_ref[k:k + 1, :] = jnp.sum(jnp.where(pick, rank_dense, 0.0), axis=0,
                                       keepdims=True).astype(jnp.int32)
        w_ref[k:k + 1, :] = svals[k] / denom * ROUTED_SCALE


def _route(logits_t, router_bias):
    ne, t = logits_t.shape
    tt = 1024
    bias = jnp.broadcast_to(router_bias.astype(F32)[:, None], (ne, LANES))
    tok = lambda i: (0, i)
    const = lambda i: (0, 0)
    return pl.pallas_call(
        _route_kernel,
        grid=(t // tt,),
        in_specs=[pl.BlockSpec((ne, tt), tok), pl.BlockSpec((ne, LANES), const)],
        out_specs=[
            pl.BlockSpec((TOP_K, tt), tok),
            pl.BlockSpec((TOP_K, tt), tok),
            pl.BlockSpec((TOP_K, tt), tok),
            pl.BlockSpec((ne, LANES), const),
        ],
        out_shape=[
            jax.ShapeDtypeStruct((TOP_K, t), jnp.int32),
            jax.ShapeDtypeStruct((TOP_K, t), jnp.int32),
            jax.ShapeDtypeStruct((TOP_K, t), F32),
            jax.ShapeDtypeStruct((ne, LANES), F32),
        ],
        scratch_shapes=[pltpu.VMEM((tt, tt), BF16), pltpu.VMEM((ne, LANES), F32)],
        compiler_params=_cp(("arbitrary",)),
        name="moe_route",
    )(logits_t, bias)


def _dest_kernel(pstart_ref, eidx_ref, rank_ref, dest_ref):
    e = eidx_ref[...]
    base = jnp.zeros_like(e)
    for x in range(N_EXPERTS):
        base = jnp.where(e == x, pstart_ref[x], base)
    dest_ref[...] = rank_ref[...] + base


def _dest_rows(pstart, eidx, rank):
    return pl.pallas_call(
        _dest_kernel,
        in_specs=[pl.BlockSpec(memory_space=pltpu.SMEM),
                  pl.BlockSpec(memory_space=pltpu.VMEM),
                  pl.BlockSpec(memory_space=pltpu.VMEM)],
        out_specs=pl.BlockSpec(memory_space=pltpu.VMEM),
        out_shape=jax.ShapeDtypeStruct(eidx.shape, jnp.int32),
        name="moe_dest",
    )(pstart, eidx, rank)


def _dispatch_kernel(pend_ref, has_ref, dest_ref, hp_ref, xs_hbm, zbuf, sem, zsem):
    i = pl.program_id(0)
    td = hp_ref.shape[0]
    bm = zbuf.shape[0]

    @pl.when(i == 0)
    def _zero_tails():
        zbuf[...] = jnp.zeros_like(zbuf)

        def tail(e):
            start_row = pl.multiple_of(pend_ref[e] - bm, bm)
            return pltpu.make_async_copy(zbuf, xs_hbm.at[pl.ds(start_row, bm)], zsem)

        def start(e, carry):
            @pl.when(has_ref[e] == 1)
            def _():
                tail(e).start()
            return carry

        def wait(e, carry):
            @pl.when(has_ref[e] == 1)
            def _():
                tail(e).wait()
            return carry

        lax.fori_loop(0, N_EXPERTS, start, 0)
        lax.fori_loop(0, N_EXPERTS, wait, 0)

    def row(tok, k):
        return pltpu.make_async_copy(hp_ref.at[pl.ds(tok, 1)],
                                     xs_hbm.at[pl.ds(dest_ref[k * td + tok], 1)], sem)

    def start_rows(g, carry):
        for j in range(SUBLANES):
            for k in range(TOP_K):
                row(g * SUBLANES + j, k).start(priority=k % 2)
        return carry

    def wait_rows(g, carry):
        for j in range(SUBLANES):
            for k in range(TOP_K):
                row(g * SUBLANES + j, k).wait()
        return carry

    lax.fori_loop(0, td // SUBLANES, start_rows, 0)
    lax.fori_loop(0, td // SUBLANES, wait_rows, 0)


def _dispatch(hp, dest_tiles, pends, has, n_rows, td):
    t = hp.shape[0]
    tile = hp.shape[1:]
    grid_spec = pltpu.PrefetchScalarGridSpec(
        num_scalar_prefetch=2,
        grid=(t // td,),
        in_specs=[
            pl.BlockSpec((TOP_K * td,), lambda i, pe, ha: (i,), memory_space=pltpu.SMEM),
            pl.BlockSpec((td,) + tile, lambda i, pe, ha: (i, 0, 0)),
        ],
        out_specs=pl.BlockSpec(memory_space=pl.ANY),
        scratch_shapes=[pltpu.VMEM((MOE_BM,) + tile, jnp.uint32), pltpu.SemaphoreType.DMA(()),
                        pltpu.SemaphoreType.DMA(())],
    )
    return pl.pallas_call(
        _dispatch_kernel,
        grid_spec=grid_spec,
        out_shape=jax.ShapeDtypeStruct((n_rows,) + tile, jnp.uint32),
        compiler_params=_cp(("arbitrary",)),
        name="moe_dispatch",
    )(pends, has, dest_tiles, hp)


def _experts_kernel(blk_e_ref, blk_v_ref, xs_ref, wg_ref, wu_ref, wd_ref, o_ref, wgb, wub, wdb):
    b = pl.program_id(0)
    valid = blk_v_ref[b] == 1
    changed = (b == 0) | (blk_e_ref[b] != blk_e_ref[jnp.maximum(b - 1, 0)])

    @pl.when(valid & changed)
    def _cast_weights():
        wgb[...] = wg_ref[...].astype(BF16)
        wub[...] = wu_ref[...].astype(BF16)
        wdb[...] = wd_ref[...].astype(BF16)

    @pl.when(valid)
    def _compute():
        y = _swiglu_packed(_rows_from_tiles(xs_ref[...]), wgb[...], wub[...], wdb[...])
        o_ref[...] = _tiles_from_rows(_pack_rows(y))

    @pl.when(jnp.logical_not(valid))
    def _skip():
        o_ref[...] = jnp.zeros_like(o_ref)


def _experts(xs, blk_e, blk_v, wg, wu, wd, layer):
    _, ne, d, ff = wg.shape
    bm = MOE_BM
    nblk = blk_e.shape[0]
    tile = xs.shape[1:]
    wmap = lambda b, be, bv: (layer, be[b], 0, 0)
    grid_spec = pltpu.PrefetchScalarGridSpec(
        num_scalar_prefetch=2,
        grid=(nblk,),
        in_specs=[
            pl.BlockSpec((bm,) + tile, lambda b, be, bv: (jnp.where(bv[b] == 1, b, 0), 0, 0)),
            pl.BlockSpec((None, None, d, ff), wmap),
            pl.BlockSpec((None, None, d, ff), wmap),
            pl.BlockSpec((None, None, ff, d), wmap),
        ],
        out_specs=pl.BlockSpec((bm,) + tile, lambda b, be, bv: (b, 0, 0)),
        scratch_shapes=[pltpu.VMEM((d, ff), BF16), pltpu.VMEM((d, ff), BF16), pltpu.VMEM((ff, d), BF16)],
    )
    return pl.pallas_call(
        _experts_kernel,
        grid_spec=grid_spec,
        out_shape=jax.ShapeDtypeStruct((nblk * bm,) + tile, jnp.uint32),
        compiler_params=_cp(("arbitrary",)),
        name="moe_experts",
    )(blk_e, blk_v, xs, wg, wu, wd)


def _shared_kernel(hp_ref, wg_ref, wu_ref, wd_ref, o_ref):
    o_ref[...] = _swiglu_packed(_rows_from_tiles(hp_ref[...]), wg_ref[...], wu_ref[...], wd_ref[...])


def _shared(hp, wg, wu, wd):
    t = hp.shape[0]
    d, ff = wg.shape
    tm = 512
    const = lambda i: (0, 0)
    return pl.pallas_call(
        _shared_kernel,
        grid=(t // tm,),
        in_specs=[
            pl.BlockSpec((tm,) + hp.shape[1:], lambda i: (i, 0, 0)),
            pl.BlockSpec((d, ff), const),
            pl.BlockSpec((d, ff), const),
            pl.BlockSpec((ff, d), const),
        ],
        out_specs=pl.BlockSpec((tm, d), lambda i: (i, 0)),
        out_shape=jax.ShapeDtypeStruct((t, d), F32),
        compiler_params=_cp(("arbitrary",)),
        name="moe_shared",
    )(hp, wg, wu, wd)


def _combine_kernel(dcur_ref, dnxt_ref, y_hbm, wts_ref, sh_ref, x_ref, mod_ref, nmod_ref, lng_ref,
                    lnb_ref, *rest, last):
    if last:
        xo_ref, ybuf, sem = rest
    else:
        xo_ref, ho_ref, ybuf, sem = rest
    tt = x_ref.shape[0]
    i = pl.program_id(0)
    slot = i % 2

    def row(dref, slot_, tok, k):
        return pltpu.make_async_copy(y_hbm.at[pl.ds(dref[k * tt + tok], 1)],
                                     ybuf.at[slot_, pl.ds(k * tt + tok, 1)], sem.at[slot_])

    grp = 2 * SUBLANES
    ngrp = tt // grp

    def tile_loop(dref, slot_, op):
        def body(g, carry):
            for j in range(grp):
                for k in range(TOP_K):
                    copy = row(dref, slot_, g * grp + j, k)
                    if op == "start":
                        copy.start(priority=k % 2)
                    else:
                        copy.wait()
            return carry

        lax.fori_loop(0, ngrp, body, 0)

    @pl.when(i == 0)
    def _prime():
        tile_loop(dcur_ref, 0, "start")

    @pl.when(i + 1 < pl.num_programs(0))
    def _prefetch():
        tile_loop(dnxt_ref, 1 - slot, "start")

    tile_loop(dcur_ref, slot, "wait")

    wb = [jnp.broadcast_to(wts_ref[:, k:k + 1], (tt, LANES)) for k in range(TOP_K)]
    lo_cols = [None] * SUBLANES
    hi_cols = [None] * SUBLANES
    for k in range(TOP_K):
        yt = jnp.transpose(ybuf[slot, pl.ds(k * tt, tt)], (1, 0, 2))
        for s in range(SUBLANES):
            lo, hi = _unpack_f32(yt[s])
            lo_cols[s] = lo * wb[k] if k == 0 else lo_cols[s] + lo * wb[k]
            hi_cols[s] = hi * wb[k] if k == 0 else hi_cols[s] + hi * wb[k]
    routed = jnp.concatenate(lo_cols + hi_cols, axis=1)
    xn = _resid_ln(routed + sh_ref[...], x_ref[...], mod_ref, lng_ref, lnb_ref)
    xo_ref[...] = xn
    if not last:
        ho_ref[...] = (xn * (1.0 + nmod_ref[1:2, :]) + nmod_ref[0:1, :]).astype(ho_ref.dtype)


def _combine(ys, dest_tiles, wts, shared, x2, mods, sub, ln_g, ln_b, seq, last):
    t, d = x2.shape
    tt = COMB_TT
    nt = t // tt
    per_b = seq // tt
    nsub = sub if last else sub + 1
    row = lambda i: (i, 0)
    const = lambda i: (0, 0)
    outs = pl.pallas_call(
        functools.partial(_combine_kernel, last=last),
        grid=(nt,),
        in_specs=[
            pl.BlockSpec((tt * TOP_K,), lambda i: (i,), memory_space=pltpu.SMEM),
            pl.BlockSpec((tt * TOP_K,), lambda i: (jnp.minimum(i + 1, nt - 1),), memory_space=pltpu.SMEM),
            pl.BlockSpec(memory_space=pl.ANY),
            pl.BlockSpec((tt, TOP_K), row),
            pl.BlockSpec((tt, d), row),
            pl.BlockSpec((tt, d), row),
            pl.BlockSpec((None, None, 3, d), lambda i: (sub, i // per_b, 0, 0)),
            pl.BlockSpec((None, None, 3, d), lambda i: (nsub, i // per_b, 0, 0)),
            pl.BlockSpec((1, d), const),
            pl.BlockSpec((1, d), const),
        ],
        out_specs=[pl.BlockSpec((tt, d), row)] * (1 if last else 2),
        out_shape=[jax.ShapeDtypeStruct((t, d), F32)] + ([] if last else [jax.ShapeDtypeStruct((t, d), BF16)]),
        scratch_shapes=[pltpu.VMEM((2, TOP_K * tt) + ys.shape[1:], jnp.uint32),
                        pltpu.SemaphoreType.DMA((2,))],
        compiler_params=_cp(("arbitrary",)),
        name="moe_combine",
    )(dest_tiles, dest_tiles, ys, wts, shared, x2, mods, mods, ln_g.reshape(1, d), ln_b.reshape(1, d))
    return (outs[0], None) if last else (outs[0], outs[1])


def _tile_major(a, tile):
    k, t = a.shape
    return a.reshape(k, t // tile, tile).transpose(1, 0, 2).reshape(-1)


def _moe(hp, logits_t, x2, mods, sub, p, layer, seq, last):
    t = hp.shape[0]
    bm = MOE_BM
    eidx, rank, w_t, cnt = _route(logits_t, p["router_bias"])
    counts = cnt[:, 0].astype(jnp.int32)
    padded = (counts + bm - 1) // bm * bm
    pends = jnp.cumsum(padded)
    pstart = pends - padded
    nblk = (t * TOP_K) // bm + N_EXPERTS
    blk_start = jnp.arange(nblk, dtype=jnp.int32) * bm
    blk_e = jnp.minimum(jnp.sum(blk_start[:, None] >= pends[None, :], axis=1), N_EXPERTS - 1).astype(jnp.int32)
    blk_v = (blk_start < pends[-1]).astype(jnp.int32)
    has = (counts > 0).astype(jnp.int32)

    dest = _dest_rows(pstart.astype(jnp.int32), eidx, rank)
    xs = _dispatch(hp, _tile_major(dest, DISP_TD), pends.astype(jnp.int32), has, nblk * bm, DISP_TD)
    ys = _experts(xs, blk_e, blk_v, p["w_gate"], p["w_up"], p["w_down"], layer)
    sh = _shared(hp, p["ws_gate"].astype(BF16), p["ws_up"].astype(BF16), p["ws_down"].astype(BF16))
    return _combine(ys, _tile_major(dest, COMB_TT), jnp.transpose(w_t), sh, x2, mods, sub,
                    p["ln_g"], p["ln_b"], seq, last)


def kernel(x, c, ada_w, ada_b, ln_g, ln_b, gm_w_in, gm_b_in, gm_v_ln_g, gm_v_ln_b, gm_w_s, gm_b_s,
           gm_w_out, gm_b_out, at_w_qkv, at_b_qkv, at_sinks, at_w_o, at_b_o, moe_w_router,
           moe_router_bias, moe_w_gate, moe_w_up, moe_w_down, moe_ws_gate, moe_ws_up, moe_ws_down):
    bsz, seq, d = x.shape
    t = bsz * seq
    depth = ada_w.shape[0]
    x2 = x.reshape(t, d)

    rows = 8
    c_pad = jnp.zeros((rows, d), F32).at[:bsz].set(c)
    mods = _adamod(c_pad, ada_w.reshape(depth * 2, d, 3 * d), ada_b.reshape(depth * 2, 3 * d))
    mods = mods.reshape(depth * 2, rows, 3, d)

    h = _modulate(x2, mods, 0, seq)
    for i in range(depth):
        j = i // 2
        wrt = moe_w_router[i]
        if i % 2 == 0:
            z = _mm(h, gm_w_in[j].astype(BF16), gm_b_in[j], act="gelu", tm=512, tn=1024, out_dtype=BF16)
            a = _gmlp_gate(z, gm_v_ln_g[j], gm_v_ln_b[j], gm_w_s[j], gm_b_s[j])
            x2, hp, logits_t = _proj_resid_ln(a, gm_w_out[j].astype(BF16), gm_b_out[j], x2, mods, 2 * i,
                                              ln_g[i, 0], ln_b[i, 0], wrt, seq)
        else:
            qkv = _mm(h, at_w_qkv[j].astype(BF16), at_b_qkv[j], act=None, tm=512, tn=1280, out_dtype=BF16)
            o = _attention(qkv, at_sinks[j], seq)
            x2, hp, logits_t = _proj_resid_ln(o, at_w_o[j].astype(BF16), at_b_o[j], x2, mods, 2 * i,
                                              ln_g[i, 0], ln_b[i, 0], wrt, seq)
        p = dict(router_bias=moe_router_bias[i], w_gate=moe_w_gate, w_up=moe_w_up, w_down=moe_w_down,
                 ws_gate=moe_ws_gate[i], ws_up=moe_ws_up[i], ws_down=moe_ws_down[i],
                 ln_g=ln_g[i, 1], ln_b=ln_b[i, 1])
        x2, h = _moe(hp, logits_t, x2, mods, 2 * i + 1, p, i, seq, last=(i == depth - 1))
    return x2.reshape(bsz, seq, d)
```

```python
import functools
import math

import numpy as np
import jax
import jax.numpy as jnp
from jax import lax
from jax.experimental import pallas as pl
from jax.experimental.pallas import tpu as pltpu

F32 = jnp.float32
BF16 = jnp.bfloat16

DEPTH = 4
CHUNK = 128
GMLP_GROUPS = 8
HEAD_DIM = 64
Q_PER_KV = 8
ATT_BLOCK = 128
N_EXPERTS = 64
TOP_K = 8
N_EXPERT_GROUPS = 8
TOPK_GROUPS = 4
ROUTED_SCALE = 2.5
DEEPNORM_ALPHA = (2.0 * DEPTH) ** 0.25
LN_EPS = 1e-5
NEG_INF = -1e30

LANES = 128
VMEM_LIMIT = 56 * 1024 * 1024

MOE_BM = 512
COMB_TT = 256
DISP_TD = 256


def _cp(sem, vmem=VMEM_LIMIT):
    return pltpu.CompilerParams(dimension_semantics=sem, vmem_limit_bytes=vmem)


def _layer_norm(x, g, b):
    mu = jnp.mean(x, axis=-1, keepdims=True)
    xc = x - mu
    var = jnp.mean(xc * xc, axis=-1, keepdims=True)
    return xc * lax.rsqrt(var + LN_EPS) * g + b


def _alibi_slopes(n):
    p = 2 ** int(math.floor(math.log2(n)))
    base = [2.0 ** (-8.0 * (i + 1) / p) for i in range(p)]
    extra = [2.0 ** (-4.0 * (2 * i + 1) / p) for i in range(n - p)]
    return [float(np.float32(s)) for s in base + extra]


def _adamod_kernel(c_ref, w_ref, b_ref, o_ref):
    c = c_ref[...]
    ca = (c * jax.nn.sigmoid(c)).astype(BF16)
    o_ref[...] = jnp.dot(ca, w_ref[...].astype(BF16), preferred_element_type=F32) + b_ref[...]


def _adamod(c_pad, ada_w, ada_b):
    ns, d, d3 = ada_w.shape
    rows = c_pad.shape[0]
    tn = 768
    return pl.pallas_call(
        _adamod_kernel,
        grid=(ns, d3 // tn),
        in_specs=[
            pl.BlockSpec((rows, d), lambda s, j: (0, 0)),
            pl.BlockSpec((None, d, tn), lambda s, j: (s, 0, j)),
            pl.BlockSpec((None, 1, tn), lambda s, j: (s, 0, j)),
        ],
        out_specs=pl.BlockSpec((None, rows, tn), lambda s, j: (s, 0, j)),
        out_shape=jax.ShapeDtypeStruct((ns, rows, d3), F32),
        compiler_params=_cp(("arbitrary", "arbitrary")),
        name="adamod",
    )(c_pad, ada_w, ada_b.reshape(ns, 1, d3))


def _modulate_kernel(x_ref, mod_ref, h_ref):
    h = x_ref[...] * (1.0 + mod_ref[1:2, :]) + mod_ref[0:1, :]
    h_ref[...] = h.astype(h_ref.dtype)


def _modulate(x2, mods, sub, seq):
    t, d = x2.shape
    tm = 512
    per_b = seq // tm
    return pl.pallas_call(
        _modulate_kernel,
        grid=(t // tm,),
        in_specs=[
            pl.BlockSpec((tm, d), lambda i: (i, 0)),
            pl.BlockSpec((None, None, 3, d), lambda i: (sub, i // per_b, 0, 0)),
        ],
        out_specs=pl.BlockSpec((tm, d), lambda i: (i, 0)),
        out_shape=jax.ShapeDtypeStruct((t, d), BF16),
        compiler_params=_cp(("arbitrary",)),
        name="modulate",
    )(x2, mods)


def _mm_kernel(a_ref, w_ref, b_ref, o_ref, wb_ref, *, act):
    @pl.when(pl.program_id(1) == 0)
    def _cast_weights():
        wb_ref[...] = w_ref[...].astype(BF16)

    acc = jnp.dot(a_ref[...], wb_ref[...], preferred_element_type=F32) + b_ref[...]
    if act == "gelu":
        acc = 0.5 * acc * (1.0 + lax.erf(acc * np.float32(math.sqrt(0.5))))
    o_ref[...] = acc.astype(o_ref.dtype)


def _mm(a, w_stack, layer, b, *, act, tm, tn, out_dtype):
    m, k = a.shape
    n = w_stack.shape[2]
    return pl.pallas_call(
        functools.partial(_mm_kernel, act=act),
        grid=(n // tn, m // tm),
        in_specs=[
            pl.BlockSpec((tm, k), lambda j, i: (i, 0)),
            pl.BlockSpec((None, k, tn), lambda j, i: (layer, 0, j)),
            pl.BlockSpec((1, tn), lambda j, i: (0, j)),
        ],
        out_specs=pl.BlockSpec((tm, tn), lambda j, i: (i, j)),
        out_shape=jax.ShapeDtypeStruct((m, n), out_dtype),
        scratch_shapes=[pltpu.VMEM((k, tn), BF16)],
        compiler_params=_cp(("arbitrary", "arbitrary")),
        name="mm_" + (act or "bias"),
    )(a, w_stack, b.reshape(1, n))


def _gate_kernel(u_ref, v_ref, g_ref, b_ref, ws_ref, bst_ref, o_ref, *, nchunk, gw):
    vn = _layer_norm(v_ref[...].astype(F32), g_ref[...], b_ref[...]).astype(BF16)
    row = lax.broadcasted_iota(jnp.int32, (CHUNK, CHUNK), 0)
    col = lax.broadcasted_iota(jnp.int32, (CHUNK, CHUNK), 1)
    causal = row >= col
    for g in range(GMLP_GROUPS):
        wsg = jnp.where(causal, ws_ref[g], 0.0).astype(BF16)
        bs = bst_ref[:, g:g + 1]
        for c in range(nchunk):
            rs = slice(c * CHUNK, (c + 1) * CHUNK)
            cs = slice(g * gw, (g + 1) * gw)
            s = jnp.dot(wsg, vn[rs, cs], preferred_element_type=F32) + bs
            o_ref[rs, cs] = (u_ref[rs, cs].astype(F32) * s).astype(o_ref.dtype)


def _gmlp_gate(z, ln_g, ln_b, w_s, b_s):
    t, n2 = z.shape
    half = n2 // 2
    tm = 2 * CHUNK
    gw = half // GMLP_GROUPS
    return pl.pallas_call(
        functools.partial(_gate_kernel, nchunk=tm // CHUNK, gw=gw),
        grid=(t // tm,),
        in_specs=[
            pl.BlockSpec((tm, half), lambda i: (i, 0)),
            pl.BlockSpec((tm, half), lambda i: (i, 1)),
            pl.BlockSpec((1, half), lambda i: (0, 0)),
            pl.BlockSpec((1, half), lambda i: (0, 0)),
            pl.BlockSpec((GMLP_GROUPS, CHUNK, CHUNK), lambda i: (0, 0, 0)),
            pl.BlockSpec((CHUNK, GMLP_GROUPS), lambda i: (0, 0)),
        ],
        out_specs=pl.BlockSpec((tm, half), lambda i: (i, 0)),
        out_shape=jax.ShapeDtypeStruct((t, half), BF16),
        compiler_params=_cp(("arbitrary",)),
        name="gmlp_gate",
    )(z, z, ln_g.reshape(1, half), ln_b.reshape(1, half), w_s, jnp.transpose(b_s))


def _attn_kernel(sink_ref, q_ref, kc_ref, vc_ref, kp_ref, vp_ref, o_ref, bias_ref, p_ref,
                 *, nb, n_kv, slopes):
    blk = ATT_BLOCK
    pairs = Q_PER_KV // 2
    i = pl.program_id(0)

    @pl.when(i == 0)
    def _build_bias():
        qi = lax.broadcasted_iota(jnp.int32, (blk, 2 * blk), 0)
        kj = lax.broadcasted_iota(jnp.int32, (blk, 2 * blk), 1)
        dist = qi + blk - kj
        valid = (dist >= 0) & (dist < blk)
        distf = dist.astype(F32)
        for g in range(n_kv):
            for p in range(pairs):
                for j in range(2):
                    slope = slopes[g * Q_PER_KV + 2 * p + j]
                    base = jnp.where(valid, -slope * distf, NEG_INF)
                    rs = slice(p * blk, (p + 1) * blk)
                    cs = slice(j * 2 * blk, (j + 1) * 2 * blk)
                    bias_ref[0, g, rs, cs] = base
                    bias_ref[1, g, rs, cs] = jnp.where(kj < blk, NEG_INF, base)

    first = (i % nb == 0).astype(jnp.int32)
    lane = lax.broadcasted_iota(jnp.int32, (2 * blk, LANES), 1)
    lo = lane < HEAD_DIM
    lo_q = lax.broadcasted_iota(jnp.int32, (blk, LANES), 1) < HEAD_DIM

    def blockdiag(prev_ref, cur_ref, g, scale):
        c, half = g // 2, g % 2
        cs = slice(c * LANES, (c + 1) * LANES)
        t2 = jnp.concatenate([prev_ref[:, cs], cur_ref[:, cs]], axis=0).astype(F32) * scale
        rolled = pltpu.roll(t2, HEAD_DIM, 1)
        if half == 0:
            a = jnp.where(lo, t2, 0.0)
            b = jnp.where(lo, 0.0, rolled)
        else:
            a = jnp.where(lo, rolled, 0.0)
            b = jnp.where(lo, 0.0, t2)
        return jnp.concatenate([a, b], axis=0).astype(BF16)

    for g in range(n_kv):
        kbd = blockdiag(kp_ref, kc_ref, g, HEAD_DIM ** -0.5)
        vbd = blockdiag(vp_ref, vc_ref, g, 1.0)
        qg = jnp.concatenate(
            [q_ref[:, (g * pairs + p) * LANES:(g * pairs + p + 1) * LANES] for p in range(pairs)], axis=0)
        s = lax.dot_general(qg, kbd, (((1,), (1,)), ((), ())), preferred_element_type=F32)
        s = s + bias_ref[first, g]
        inv = []
        for p in range(pairs):
            for j in range(2):
                rs = slice(p * blk, (p + 1) * blk)
                cs = slice(j * 2 * blk, (j + 1) * 2 * blk)
                sub = s[rs, cs]
                sink = sink_ref[g * Q_PER_KV + 2 * p + j]
                m = jnp.maximum(jnp.max(sub, axis=-1, keepdims=True), sink)
                e = jnp.exp(sub - m)
                denom = jnp.sum(e, axis=-1, keepdims=True) + jnp.exp(sink - m)
                p_ref[rs, cs] = e.astype(BF16)
                inv.append(1.0 / denom)
        o = jnp.dot(p_ref[...], vbd, preferred_element_type=F32)
        for p in range(pairs):
            scale = jnp.where(lo_q, inv[2 * p], inv[2 * p + 1])
            col = (g * pairs + p) * LANES
            o_ref[:, col:col + LANES] = (o[p * blk:(p + 1) * blk] * scale).astype(o_ref.dtype)


def _attention(qkv, sinks, seq):
    t, w = qkv.shape
    blk = ATT_BLOCK
    nb = seq // blk
    n_kv = (w // HEAD_DIM) // (Q_PER_KV + 2)
    dq = n_kv * Q_PER_KV * HEAD_DIM
    dkv = n_kv * HEAD_DIM
    kcol = dq // dkv
    pairs = Q_PER_KV // 2
    slopes = _alibi_slopes(n_kv * Q_PER_KV)

    def prev(i):
        return jnp.where(i % nb == 0, i, i - 1)

    return pl.pallas_call(
        functools.partial(_attn_kernel, nb=nb, n_kv=n_kv, slopes=slopes),
        grid=(t // blk,),
        in_specs=[
            pl.BlockSpec(memory_space=pltpu.SMEM),
            pl.BlockSpec((blk, dq), lambda i: (i, 0)),
            pl.BlockSpec((blk, dkv), lambda i: (i, kcol)),
            pl.BlockSpec((blk, dkv), lambda i: (i, kcol + 1)),
            pl.BlockSpec((blk, dkv), lambda i: (prev(i), kcol)),
            pl.BlockSpec((blk, dkv), lambda i: (prev(i), kcol + 1)),
        ],
        out_specs=pl.BlockSpec((blk, dq), lambda i: (i, 0)),
        out_shape=jax.ShapeDtypeStruct((t, dq), BF16),
        scratch_shapes=[
            pltpu.VMEM((2, n_kv, pairs * blk, 4 * blk), F32),
            pltpu.VMEM((pairs * blk, 4 * blk), BF16),
        ],
        compiler_params=_cp(("arbitrary",)),
        name="swa_attention",
    )(sinks, qkv, qkv, qkv, qkv, qkv)


SUBLANES = 8


def _rows_from_tiles(x3):
    xt = jnp.transpose(x3, (1, 0, 2))
    return jnp.concatenate([xt[s] for s in range(x3.shape[1])], axis=1)


def _tiles_from_rows(x2):
    xt = jnp.stack([x2[:, s * LANES:(s + 1) * LANES] for s in range(x2.shape[1] // LANES)], axis=0)
    return jnp.transpose(xt, (1, 0, 2))


def _pack_rows(h):
    half = h.shape[1] // 2
    bits = lax.bitcast_convert_type(h.astype(BF16).astype(F32), jnp.uint32)
    return (bits[:, :half] >> 16) | (bits[:, half:] & jnp.uint32(0xFFFF0000))


def _unpack_f32(p):
    lo = lax.bitcast_convert_type(p << 16, F32)
    hi = lax.bitcast_convert_type(p & jnp.uint32(0xFFFF0000), F32)
    return lo, hi


def _unpack_rows(p):
    lo, hi = _unpack_f32(p)
    return lo.astype(BF16), hi.astype(BF16)


def _swiglu_packed(p, wg, wu, wd):
    lo, hi = _unpack_rows(p)
    half = p.shape[1]
    gt = (jnp.dot(lo, wg[:half], preferred_element_type=F32)
          + jnp.dot(hi, wg[half:], preferred_element_type=F32))
    up = (jnp.dot(lo, wu[:half], preferred_element_type=F32)
          + jnp.dot(hi, wu[half:], preferred_element_type=F32))
    act = (gt * jax.nn.sigmoid(gt) * up).astype(BF16)
    return jnp.dot(act, wd, preferred_element_type=F32)


def _resid_ln(y, x, mod_ref, lng_ref, lnb_ref):
    r = DEEPNORM_ALPHA * x + (1.0 + mod_ref[2:3, :]) * y
    return _layer_norm(r, lng_ref[...], lnb_ref[...])


def _proj_kernel(a_ref, w_ref, b_ref, x_ref, mod_ref, nmod_ref, lng_ref, lnb_ref, wrh_ref, wrl_ref,
                 xo_ref, hp_ref, lt_ref, acc_ref, *, nk, slab):
    k = pl.program_id(1)

    @pl.when(k == 0)
    def _zero():
        acc_ref[...] = jnp.zeros_like(acc_ref)

    acc_ref[...] += jnp.dot(a_ref[...], w_ref[...], preferred_element_type=F32)

    @pl.when(k == nk - 1)
    def _finish():
        for r0 in range(0, acc_ref.shape[0], slab):
            rs = slice(r0, r0 + slab)
            xn = _resid_ln(acc_ref[rs, :] + b_ref[...], x_ref[rs, :], mod_ref, lng_ref, lnb_ref)
            xo_ref[rs, :] = xn
            h = xn * (1.0 + nmod_ref[1:2, :]) + nmod_ref[0:1, :]
            hp_ref[rs] = _tiles_from_rows(_pack_rows(h))
            h_hi = h.astype(BF16)
            h_lo = (h - h_hi.astype(F32)).astype(BF16)
            lg = (jnp.dot(h_hi, wrh_ref[...], preferred_element_type=F32)
                  + (jnp.dot(h_hi, wrl_ref[...], preferred_element_type=F32)
                     + jnp.dot(h_lo, wrh_ref[...], preferred_element_type=F32)))
            lt_ref[:, rs] = jnp.transpose(lg)[:lt_ref.shape[0]]


def _proj_resid_ln(a, w, b, x2, mods, sub, ln_g, ln_b, w_router, seq):
    t, kdim = a.shape
    d = w.shape[1]
    tm = 512
    tk = min(kdim, 1024)
    nk = kdim // tk
    per_b = seq // tm
    ne = w_router.shape[1]
    wr = jnp.zeros((d, LANES), F32).at[:, :ne].set(w_router)
    wr_hi = wr.astype(BF16)
    wr_lo = (wr - wr_hi.astype(F32)).astype(BF16)
    row = lambda i, k: (i, 0)
    const = lambda i, k: (0, 0)
    return pl.pallas_call(
        functools.partial(_proj_kernel, nk=nk, slab=128),
        grid=(t // tm, nk),
        in_specs=[
            pl.BlockSpec((tm, tk), lambda i, k: (i, k)),
            pl.BlockSpec((tk, d), lambda i, k: (k, 0)),
            pl.BlockSpec((1, d), const),
            pl.BlockSpec((tm, d), row),
            pl.BlockSpec((None, None, 3, d), lambda i, k: (sub, i // per_b, 0, 0)),
            pl.BlockSpec((None, None, 3, d), lambda i, k: (sub + 1, i // per_b, 0, 0)),
            pl.BlockSpec((1, d), const),
            pl.BlockSpec((1, d), const),
            pl.BlockSpec((d, LANES), const),
            pl.BlockSpec((d, LANES), const),
        ],
        out_specs=[
            pl.BlockSpec((tm, d), row),
            pl.BlockSpec((tm, SUBLANES, LANES), lambda i, k: (i, 0, 0)),
            pl.BlockSpec((ne, tm), lambda i, k: (0, i)),
        ],
        out_shape=[
            jax.ShapeDtypeStruct((t, d), F32),
            jax.ShapeDtypeStruct((t, SUBLANES, LANES), jnp.uint32),
            jax.ShapeDtypeStruct((ne, t), F32),
        ],
        scratch_shapes=[pltpu.VMEM((tm, d), F32)],
        compiler_params=_cp(("arbitrary", "arbitrary")),
        name="proj_resid_ln",
    )(a, w, b.reshape(1, d), x2, mods, mods, ln_g.reshape(1, d), ln_b.reshape(1, d), wr_hi, wr_lo)


def _route_kernel(lg_ref, bias_ref, eidx_ref, rank_ref, w_ref, cnt_ref, tri_ref, carry_ref):
    i = pl.program_id(0)
    ne, tt = lg_ref.shape
    gsz = ne // N_EXPERT_GROUPS
    neg = -jnp.inf

    @pl.when(i == 0)
    def _init():
        r = lax.broadcasted_iota(jnp.int32, (tt, tt), 0)
        c = lax.broadcasted_iota(jnp.int32, (tt, tt), 1)
        tri_ref[...] = jnp.where(r < c, 1.0, 0.0).astype(BF16)
        carry_ref[...] = jnp.zeros_like(carry_ref)

    def amax(v, ids, limit):
        m = jnp.max(v, axis=0, keepdims=True)
        idx = jnp.min(jnp.where(v == m, ids, limit), axis=0, keepdims=True)
        return m, idx

    s = jax.nn.sigmoid(lg_ref[...])
    sel = s + bias_ref[:, 0:1]

    ids_g = lax.broadcasted_iota(jnp.int32, (gsz, tt), 0).astype(F32)
    gscore = []
    for g in range(N_EXPERT_GROUPS):
        xg = sel[g * gsz:(g + 1) * gsz]
        m1, i1 = amax(xg, ids_g, float(gsz))
        m2 = jnp.max(jnp.where(ids_g == i1, neg, xg), axis=0, keepdims=True)
        gscore.append(m1 + m2)
    work = jnp.concatenate(gscore, axis=0)
    ids_n = lax.broadcasted_iota(jnp.int32, (N_EXPERT_GROUPS, tt), 0).astype(F32)
    gsel = jnp.zeros((N_EXPERT_GROUPS, tt), F32)
    for _ in range(TOPK_GROUPS):
        _, ig = amax(work, ids_n, float(N_EXPERT_GROUPS))
        pick = ids_n == ig
        gsel = jnp.where(pick, 1.0, gsel)
        work = jnp.where(pick, neg, work)
    work = jnp.concatenate(
        [jnp.where(gsel[g:g + 1] > 0.5, sel[g * gsz:(g + 1) * gsz], NEG_INF)
         for g in range(N_EXPERT_GROUPS)], axis=0)

    ids_e = lax.broadcasted_iota(jnp.int32, (ne, tt), 0).astype(F32)
    chosen = jnp.zeros((ne, tt), F32)
    picked, svals = [], []
    for _ in range(TOP_K):
        _, ik = amax(work, ids_e, float(ne))
        pick = ids_e == ik
        chosen = jnp.where(pick, 1.0, chosen)
        work = jnp.where(pick, neg, work)
        picked.append(ik)
        svals.append(jnp.sum(jnp.where(pick, s, 0.0), axis=0, keepdims=True))
    denom = svals[0]
    for k in range(1, TOP_K):
        denom = denom + svals[k]

    prefix = jnp.dot(chosen.astype(BF16), tri_ref[...], preferred_element_type=F32)
    rank_dense = prefix + carry_ref[:, 0:1]
    carry_ref[...] = carry_ref[...] + jnp.sum(chosen, axis=1, keepdims=True)
    cnt_ref[...] = carry_ref[...]
    for k in range(TOP_K):
        pick = ids_e == picked[k]
        eidx_ref[k:k + 1, :] = picked[k].astype(jnp.int32)
        rank_ref[k:k + 1, :] = jnp.sum(jnp.where(pick, rank_dense, 0.0), axis=0,
                                       keepdims=True).astype(jnp.int32)
        w_ref[k:k + 1, :] = svals[k] / denom * ROUTED_SCALE


def _route(logits_t, router_bias):
    ne, t = logits_t.shape
    tt = 1024
    bias = jnp.broadcast_to(router_bias.astype(F32)[:, None], (ne, LANES))
    tok = lambda i: (0, i)
    const = lambda i: (0, 0)
    return pl.pallas_call(
        _route_kernel,
        grid=(t // tt,),
        in_specs=[pl.BlockSpec((ne, tt), tok), pl.BlockSpec((ne, LANES), const)],
        out_specs=[
            pl.BlockSpec((TOP_K, tt), tok),
            pl.BlockSpec((TOP_K, tt), tok),
            pl.BlockSpec((TOP_K, tt), tok),
            pl.BlockSpec((ne, LANES), const),
        ],
        out_shape=[
            jax.ShapeDtypeStruct((TOP_K, t), jnp.int32),
            jax.ShapeDtypeStruct((TOP_K, t), jnp.int32),
            jax.ShapeDtypeStruct((TOP_K, t), F32),
            jax.ShapeDtypeStruct((ne, LANES), F32),
        ],
        scratch_shapes=[pltpu.VMEM((tt, tt), BF16), pltpu.VMEM((ne, LANES), F32)],
        compiler_params=_cp(("arbitrary",)),
        name="moe_route",
    )(logits_t, bias)


def _dest_kernel(pstart_ref, eidx_ref, rank_ref, dest_ref):
    e = eidx_ref[...]
    base = jnp.zeros_like(e)
    for x in range(N_EXPERTS):
        base = jnp.where(e == x, pstart_ref[x], base)
    dest_ref[...] = rank_ref[...] + base


def _dest_rows(pstart, eidx, rank):
    return pl.pallas_call(
        _dest_kernel,
        in_specs=[pl.BlockSpec(memory_space=pltpu.SMEM),
                  pl.BlockSpec(memory_space=pltpu.VMEM),
                  pl.BlockSpec(memory_space=pltpu.VMEM)],
        out_specs=pl.BlockSpec(memory_space=pltpu.VMEM),
        out_shape=jax.ShapeDtypeStruct(eidx.shape, jnp.int32),
        name="moe_dest",
    )(pstart, eidx, rank)


def _dispatch_kernel(pend_ref, has_ref, dest_ref, hp_ref, wg_ref, wu_ref, wd_ref, xs_hbm, sh_ref,
                     zbuf, sem, zsem):
    i = pl.program_id(0)
    td = hp_ref.shape[0]
    bm = zbuf.shape[0]

    @pl.when(i == 0)
    def _zero_tails():
        zbuf[...] = jnp.zeros_like(zbuf)

        def tail(e):
            start_row = pl.multiple_of(pend_ref[e] - bm, bm)
            return pltpu.make_async_copy(zbuf, xs_hbm.at[pl.ds(start_row, bm)], zsem)

        def start(e, carry):
            @pl.when(has_ref[e] == 1)
            def _():
                tail(e).start()
            return carry

        def wait(e, carry):
            @pl.when(has_ref[e] == 1)
            def _():
                tail(e).wait()
            return carry

        lax.fori_loop(0, N_EXPERTS, start, 0)
        lax.fori_loop(0, N_EXPERTS, wait, 0)

    def row(tok, k):
        return pltpu.make_async_copy(hp_ref.at[pl.ds(tok, 1)],
                                     xs_hbm.at[pl.ds(dest_ref[k * td + tok], 1)], sem)

    def start_rows(g, carry):
        for j in range(SUBLANES):
            for k in range(TOP_K):
                row(g * SUBLANES + j, k).start(priority=k % 2)
        return carry

    def wait_rows(g, carry):
        for j in range(SUBLANES):
            for k in range(TOP_K):
                row(g * SUBLANES + j, k).wait()
        return carry

    lax.fori_loop(0, td // SUBLANES, start_rows, 0)
    sh_ref[...] = _swiglu_packed(_rows_from_tiles(hp_ref[...]), wg_ref[...], wu_ref[...], wd_ref[...])
    lax.fori_loop(0, td // SUBLANES, wait_rows, 0)


def _dispatch(hp, dest_tiles, pends, has, n_rows, td, wg, wu, wd):
    t = hp.shape[0]
    tile = hp.shape[1:]
    d, ff = wg.shape
    const = lambda i, pe, ha: (0, 0)
    grid_spec = pltpu.PrefetchScalarGridSpec(
        num_scalar_prefetch=2,
        grid=(t // td,),
        in_specs=[
            pl.BlockSpec((TOP_K * td,), lambda i, pe, ha: (i,), memory_space=pltpu.SMEM),
            pl.BlockSpec((td,) + tile, lambda i, pe, ha: (i, 0, 0)),
            pl.BlockSpec((d, ff), const),
            pl.BlockSpec((d, ff), const),
            pl.BlockSpec((ff, d), const),
        ],
        out_specs=[pl.BlockSpec(memory_space=pl.ANY),
                   pl.BlockSpec((td, d), lambda i, pe, ha: (i, 0))],
        scratch_shapes=[pltpu.VMEM((MOE_BM,) + tile, jnp.uint32), pltpu.SemaphoreType.DMA(()),
                        pltpu.SemaphoreType.DMA(())],
    )
    return pl.pallas_call(
        _dispatch_kernel,
        grid_spec=grid_spec,
        out_shape=[jax.ShapeDtypeStruct((n_rows,) + tile, jnp.uint32),
                   jax.ShapeDtypeStruct((t, d), F32)],
        compiler_params=_cp(("arbitrary",)),
        name="moe_dispatch",
    )(pends, has, dest_tiles, hp, wg, wu, wd)


def _experts_kernel(blk_e_ref, blk_v_ref, xs_ref, wg_ref, wu_ref, wd_ref, o_ref, wgb, wub, wdb):
    b = pl.program_id(0)
    valid = blk_v_ref[b] == 1
    changed = (b == 0) | (blk_e_ref[b] != blk_e_ref[jnp.maximum(b - 1, 0)])

    @pl.when(valid & changed)
    def _cast_weights():
        wgb[...] = wg_ref[...].astype(BF16)
        wub[...] = wu_ref[...].astype(BF16)
        wdb[...] = wd_ref[...].astype(BF16)

    @pl.when(valid)
    def _compute():
        y = _swiglu_packed(_rows_from_tiles(xs_ref[...]), wgb[...], wub[...], wdb[...])
        o_ref[...] = _tiles_from_rows(_pack_rows(y))

    @pl.when(jnp.logical_not(valid))
    def _skip():
        o_ref[...] = jnp.zeros_like(o_ref)


def _experts(xs, blk_e, blk_v, wg, wu, wd, layer):
    _, ne, d, ff = wg.shape
    bm = MOE_BM
    nblk = blk_e.shape[0]
    tile = xs.shape[1:]
    wmap = lambda b, be, bv: (layer, be[b], 0, 0)
    grid_spec = pltpu.PrefetchScalarGridSpec(
        num_scalar_prefetch=2,
        grid=(nblk,),
        in_specs=[
            pl.BlockSpec((bm,) + tile, lambda b, be, bv: (jnp.where(bv[b] == 1, b, 0), 0, 0)),
            pl.BlockSpec((None, None, d, ff), wmap),
            pl.BlockSpec((None, None, d, ff), wmap),
            pl.BlockSpec((None, None, ff, d), wmap),
        ],
        out_specs=pl.BlockSpec((bm,) + tile, lambda b, be, bv: (b, 0, 0)),
        scratch_shapes=[pltpu.VMEM((d, ff), BF16), pltpu.VMEM((d, ff), BF16), pltpu.VMEM((ff, d), BF16)],
    )
    return pl.pallas_call(
        _experts_kernel,
        grid_spec=grid_spec,
        out_shape=jax.ShapeDtypeStruct((nblk * bm,) + tile, jnp.uint32),
        compiler_params=_cp(("arbitrary",)),
        name="moe_experts",
    )(blk_e, blk_v, xs, wg, wu, wd)


def _combine_kernel(dcur_ref, dnxt_ref, y_hbm, wts_ref, sh_ref, x_ref, mod_ref, nmod_ref, lng_ref,
                    lnb_ref, *rest, last):
    if last:
        xo_ref, ybuf, sem = rest
    else:
        xo_ref, ho_ref, ybuf, sem = rest
    tt = x_ref.shape[0]
    i = pl.program_id(0)
    slot = i % 2

    def row(dref, slot_, tok, k):
        return pltpu.make_async_copy(y_hbm.at[pl.ds(dref[k * tt + tok], 1)],
                                     ybuf.at[slot_, pl.ds(k * tt + tok, 1)], sem.at[slot_])

    grp = 2 * SUBLANES
    ngrp = tt // grp

    def tile_loop(dref, slot_, op):
        def body(g, carry):
            for j in range(grp):
                for k in range(TOP_K):
                    copy = row(dref, slot_, g * grp + j, k)
                    if op == "start":
                        copy.start(priority=k % 2)
                    else:
                        copy.wait()
            return carry

        lax.fori_loop(0, ngrp, body, 0)

    @pl.when(i == 0)
    def _prime():
        tile_loop(dcur_ref, 0, "start")

    @pl.when(i + 1 < pl.num_programs(0))
    def _prefetch():
        tile_loop(dnxt_ref, 1 - slot, "start")

    tile_loop(dcur_ref, slot, "wait")

    wb = [jnp.broadcast_to(wts_ref[:, k:k + 1], (tt, LANES)) for k in range(TOP_K)]
    lo_cols = [None] * SUBLANES
    hi_cols = [None] * SUBLANES
    for k in range(TOP_K):
        yt = jnp.transpose(ybuf[slot, pl.ds(k * tt, tt)], (1, 0, 2))
        for s in range(SUBLANES):
            lo, hi = _unpack_f32(yt[s])
            lo_cols[s] = lo * wb[k] if k == 0 else lo_cols[s] + lo * wb[k]
            hi_cols[s] = hi * wb[k] if k == 0 else hi_cols[s] + hi * wb[k]
    routed = jnp.concatenate(lo_cols + hi_cols, axis=1)
    xn = _resid_ln(routed + sh_ref[...], x_ref[...], mod_ref, lng_ref, lnb_ref)
    xo_ref[...] = xn
    if not last:
        ho_ref[...] = (xn * (1.0 + nmod_ref[1:2, :]) + nmod_ref[0:1, :]).astype(ho_ref.dtype)


def _combine(ys, dest_tiles, wts, shared, x2, mods, sub, ln_g, ln_b, seq, last):
    t, d = x2.shape
    tt = COMB_TT
    nt = t // tt
    per_b = seq // tt
    nsub = sub if last else sub + 1
    row = lambda i: (i, 0)
    const = lambda i: (0, 0)
    outs = pl.pallas_call(
        functools.partial(_combine_kernel, last=last),
        grid=(nt,),
        in_specs=[
            pl.BlockSpec((tt * TOP_K,), lambda i: (i,), memory_space=pltpu.SMEM),
            pl.BlockSpec((tt * TOP_K,), lambda i: (jnp.minimum(i + 1, nt - 1),), memory_space=pltpu.SMEM),
            pl.BlockSpec(memory_space=pl.ANY),
            pl.BlockSpec((tt, TOP_K), row),
            pl.BlockSpec((tt, d), row),
            pl.BlockSpec((tt, d), row),
            pl.BlockSpec((None, None, 3, d), lambda i: (sub, i // per_b, 0, 0)),
            pl.BlockSpec((None, None, 3, d), lambda i: (nsub, i // per_b, 0, 0)),
            pl.BlockSpec((1, d), const),
            pl.BlockSpec((1, d), const),
        ],
        out_specs=[pl.BlockSpec((tt, d), row)] * (1 if last else 2),
        out_shape=[jax.ShapeDtypeStruct((t, d), F32)] + ([] if last else [jax.ShapeDtypeStruct((t, d), BF16)]),
        scratch_shapes=[pltpu.VMEM((2, TOP_K * tt) + ys.shape[1:], jnp.uint32),
                        pltpu.SemaphoreType.DMA((2,))],
        compiler_params=_cp(("arbitrary",)),
        name="moe_combine",
    )(dest_tiles, dest_tiles, ys, wts, shared, x2, mods, mods, ln_g.reshape(1, d), ln_b.reshape(1, d))
    return (outs[0], None) if last else (outs[0], outs[1])


def _tile_major(a, tile):
    k, t = a.shape
    return a.reshape(k, t // tile, tile).transpose(1, 0, 2).reshape(-1)


def _moe(hp, logits_t, x2, mods, sub, p, layer, seq, last):
    t = hp.shape[0]
    bm = MOE_BM
    eidx, rank, w_t, cnt = _route(logits_t, p["router_bias"])
    counts = cnt[:, 0].astype(jnp.int32)
    padded = (counts + bm - 1) // bm * bm
    pends = jnp.cumsum(padded)
    pstart = pends - padded
    nblk = (t * TOP_K) // bm + N_EXPERTS
    blk_start = jnp.arange(nblk, dtype=jnp.int32) * bm
    blk_e = jnp.minimum(jnp.sum(blk_start[:, None] >= pends[None, :], axis=1), N_EXPERTS - 1).astype(jnp.int32)
    blk_v = (blk_start < pends[-1]).astype(jnp.int32)
    has = (counts > 0).astype(jnp.int32)

    dest = _dest_rows(pstart.astype(jnp.int32), eidx, rank)
    xs, sh = _dispatch(hp, _tile_major(dest, DISP_TD), pends.astype(jnp.int32), has, nblk * bm, DISP_TD,
                       p["ws_gate"].astype(BF16), p["ws_up"].astype(BF16), p["ws_down"].astype(BF16))
    ys = _experts(xs, blk_e, blk_v, p["w_gate"], p["w_up"], p["w_down"], layer)
    return _combine(ys, _tile_major(dest, COMB_TT), jnp.transpose(w_t), sh, x2, mods, sub,
                    p["ln_g"], p["ln_b"], seq, last)


def kernel(x, c, ada_w, ada_b, ln_g, ln_b, gm_w_in, gm_b_in, gm_v_ln_g, gm_v_ln_b, gm_w_s, gm_b_s,
           gm_w_out, gm_b_out, at_w_qkv, at_b_qkv, at_sinks, at_w_o, at_b_o, moe_w_router,
           moe_router_bias, moe_w_gate, moe_w_up, moe_w_down, moe_ws_gate, moe_ws_up, moe_ws_down):
    bsz, seq, d = x.shape
    t = bsz * seq
    depth = ada_w.shape[0]
    x2 = x.reshape(t, d)

    rows = 8
    c_pad = jnp.zeros((rows, d), F32).at[:bsz].set(c)
    mods = _adamod(c_pad, ada_w.reshape(depth * 2, d, 3 * d), ada_b.reshape(depth * 2, 3 * d))
    mods = mods.reshape(depth * 2, rows, 3, d)

    h = _modulate(x2, mods, 0, seq)
    for i in range(depth):
        j = i // 2
        wrt = moe_w_router[i]
        if i % 2 == 0:
            z = _mm(h, gm_w_in, j, gm_b_in[j], act="gelu", tm=512, tn=1024, out_dtype=BF16)
            a = _gmlp_gate(z, gm_v_ln_g[j], gm_v_ln_b[j], gm_w_s[j], gm_b_s[j])
            x2, hp, logits_t = _proj_resid_ln(a, gm_w_out[j].astype(BF16), gm_b_out[j], x2, mods, 2 * i,
                                              ln_g[i, 0], ln_b[i, 0], wrt, seq)
        else:
            qkv = _mm(h, at_w_qkv, j, at_b_qkv[j], act=None, tm=512, tn=1280, out_dtype=BF16)
            o = _attention(qkv, at_sinks[j], seq)
            x2, hp, logits_t = _proj_resid_ln(o, at_w_o[j].astype(BF16), at_b_o[j], x2, mods, 2 * i,
                                              ln_g[i, 0], ln_b[i, 0], wrt, seq)
        p = dict(router_bias=moe_router_bias[i], w_gate=moe_w_gate, w_up=moe_w_up, w_down=moe_w_down,
                 ws_gate=moe_ws_gate[i], ws_up=moe_ws_up[i], ws_down=moe_ws_down[i],
                 ln_g=ln_g[i, 1], ln_b=ln_b[i, 1])
        x2, h = _moe(hp, logits_t, x2, mods, 2 * i + 1, p, i, seq, last=(i == depth - 1))
    return x2.reshape(bsz, seq, d)
```

```python
import functools
import math

import numpy as np
import jax
import jax.numpy as jnp
from jax import lax
from jax.experimental import pallas as pl
from jax.experimental.pallas import tpu as pltpu

F32 = jnp.float32
BF16 = jnp.bfloat16

DEPTH = 4
CHUNK = 128
GMLP_GROUPS = 8
HEAD_DIM = 64
Q_PER_KV = 8
ATT_BLOCK = 128
N_EXPERTS = 64
TOP_K = 8
N_EXPERT_GROUPS = 8
TOPK_GROUPS = 4
ROUTED_SCALE = 2.5
DEEPNORM_ALPHA = (2.0 * DEPTH) ** 0.25
LN_EPS = 1e-5
NEG_INF = -1e30

LANES = 128
VMEM_LIMIT = 56 * 1024 * 1024

MOE_BM = 512
COMB_TT = 128
DISP_TD = 256


def _cp(sem, vmem=VMEM_LIMIT):
    return pltpu.CompilerParams(dimension_semantics=sem, vmem_limit_bytes=vmem)


def _layer_norm(x, g, b):
    mu = jnp.mean(x, axis=-1, keepdims=True)
    xc = x - mu
    var = jnp.mean(xc * xc, axis=-1, keepdims=True)
    return xc * lax.rsqrt(var + LN_EPS) * g + b


def _alibi_slopes(n):
    p = 2 ** int(math.floor(math.log2(n)))
    base = [2.0 ** (-8.0 * (i + 1) / p) for i in range(p)]
    extra = [2.0 ** (-4.0 * (2 * i + 1) / p) for i in range(n - p)]
    return [float(np.float32(s)) for s in base + extra]


def _adamod_kernel(c_ref, w_ref, b_ref, o_ref):
    c = c_ref[...]
    ca = (c * jax.nn.sigmoid(c)).astype(BF16)
    o_ref[...] = jnp.dot(ca, w_ref[...].astype(BF16), preferred_element_type=F32) + b_ref[...]


def _adamod(c_pad, ada_w, ada_b):
    ns, d, d3 = ada_w.shape
    rows = c_pad.shape[0]
    tn = 768
    return pl.pallas_call(
        _adamod_kernel,
        grid=(ns, d3 // tn),
        in_specs=[
            pl.BlockSpec((rows, d), lambda s, j: (0, 0)),
            pl.BlockSpec((None, d, tn), lambda s, j: (s, 0, j)),
            pl.BlockSpec((None, 1, tn), lambda s, j: (s, 0, j)),
        ],
        out_specs=pl.BlockSpec((None, rows, tn), lambda s, j: (s, 0, j)),
        out_shape=jax.ShapeDtypeStruct((ns, rows, d3), F32),
        compiler_params=_cp(("arbitrary", "arbitrary")),
        name="adamod",
    )(c_pad, ada_w, ada_b.reshape(ns, 1, d3))


def _modulate_kernel(x_ref, mod_ref, h_ref):
    h = x_ref[...] * (1.0 + mod_ref[1:2, :]) + mod_ref[0:1, :]
    h_ref[...] = h.astype(h_ref.dtype)


def _modulate(x2, mods, sub, seq):
    t, d = x2.shape
    tm = 512
    per_b = seq // tm
    return pl.pallas_call(
        _modulate_kernel,
        grid=(t // tm,),
        in_specs=[
            pl.BlockSpec((tm, d), lambda i: (i, 0)),
            pl.BlockSpec((None, None, 3, d), lambda i: (sub, i // per_b, 0, 0)),
        ],
        out_specs=pl.BlockSpec((tm, d), lambda i: (i, 0)),
        out_shape=jax.ShapeDtypeStruct((t, d), BF16),
        compiler_params=_cp(("arbitrary",)),
        name="modulate",
    )(x2, mods)


def _mm_kernel(a_ref, w_ref, b_ref, o_ref, wb_ref, *, act):
    @pl.when(pl.program_id(1) == 0)
    def _cast_weights():
        wb_ref[...] = w_ref[...].astype(BF16)

    acc = jnp.dot(a_ref[...], wb_ref[...], preferred_element_type=F32) + b_ref[...]
    if act == "gelu":
        acc = 0.5 * acc * (1.0 + lax.erf(acc * np.float32(math.sqrt(0.5))))
    o_ref[...] = acc.astype(o_ref.dtype)


def _mm(a, w_stack, layer, b, *, act, tm, tn, out_dtype):
    m, k = a.shape
    n = w_stack.shape[2]
    return pl.pallas_call(
        functools.partial(_mm_kernel, act=act),
        grid=(n // tn, m // tm),
        in_specs=[
            pl.BlockSpec((tm, k), lambda j, i: (i, 0)),
            pl.BlockSpec((None, k, tn), lambda j, i: (layer, 0, j)),
            pl.BlockSpec((1, tn), lambda j, i: (0, j)),
        ],
        out_specs=pl.BlockSpec((tm, tn), lambda j, i: (i, j)),
        out_shape=jax.ShapeDtypeStruct((m, n), out_dtype),
        scratch_shapes=[pltpu.VMEM((k, tn), BF16)],
        compiler_params=_cp(("arbitrary", "arbitrary")),
        name="mm_" + (act or "bias"),
    )(a, w_stack, b.reshape(1, n))


def _gate_kernel(u_ref, v_ref, g_ref, b_ref, ws_ref, bst_ref, o_ref, *, nchunk, gw):
    vn = _layer_norm(v_ref[...].astype(F32), g_ref[...], b_ref[...]).astype(BF16)
    row = lax.broadcasted_iota(jnp.int32, (CHUNK, CHUNK), 0)
    col = lax.broadcasted_iota(jnp.int32, (CHUNK, CHUNK), 1)
    causal = row >= col
    for g in range(GMLP_GROUPS):
        wsg = jnp.where(causal, ws_ref[g], 0.0).astype(BF16)
        bs = bst_ref[:, g:g + 1]
        for c in range(nchunk):
            rs = slice(c * CHUNK, (c + 1) * CHUNK)
            cs = slice(g * gw, (g + 1) * gw)
            s = jnp.dot(wsg, vn[rs, cs], preferred_element_type=F32) + bs
            o_ref[rs, cs] = (u_ref[rs, cs].astype(F32) * s).astype(o_ref.dtype)


def _gmlp_gate(z, ln_g, ln_b, w_s, b_s):
    t, n2 = z.shape
    half = n2 // 2
    tm = 2 * CHUNK
    gw = half // GMLP_GROUPS
    return pl.pallas_call(
        functools.partial(_gate_kernel, nchunk=tm // CHUNK, gw=gw),
        grid=(t // tm,),
        in_specs=[
            pl.BlockSpec((tm, half), lambda i: (i, 0)),
            pl.BlockSpec((tm, half), lambda i: (i, 1)),
            pl.BlockSpec((1, half), lambda i: (0, 0)),
            pl.BlockSpec((1, half), lambda i: (0, 0)),
            pl.BlockSpec((GMLP_GROUPS, CHUNK, CHUNK), lambda i: (0, 0, 0)),
            pl.BlockSpec((CHUNK, GMLP_GROUPS), lambda i: (0, 0)),
        ],
        out_specs=pl.BlockSpec((tm, half), lambda i: (i, 0)),
        out_shape=jax.ShapeDtypeStruct((t, half), BF16),
        compiler_params=_cp(("arbitrary",)),
        name="gmlp_gate",
    )(z, z, ln_g.reshape(1, half), ln_b.reshape(1, half), w_s, jnp.transpose(b_s))


def _attn_kernel(sink_ref, q_ref, kc_ref, vc_ref, kp_ref, vp_ref, o_ref, bias_ref, p_ref,
                 *, nb, n_kv, slopes):
    blk = ATT_BLOCK
    pairs = Q_PER_KV // 2
    i = pl.program_id(0)

    @pl.when(i == 0)
    def _build_bias():
        qi = lax.broadcasted_iota(jnp.int32, (blk, 2 * blk), 0)
        kj = lax.broadcasted_iota(jnp.int32, (blk, 2 * blk), 1)
        dist = qi + blk - kj
        valid = (dist >= 0) & (dist < blk)
        distf = dist.astype(F32)
        for g in range(n_kv):
            for p in range(pairs):
                for j in range(2):
                    slope = slopes[g * Q_PER_KV + 2 * p + j]
                    base = jnp.where(valid, -slope * distf, NEG_INF)
                    rs = slice(p * blk, (p + 1) * blk)
                    cs = slice(j * 2 * blk, (j + 1) * 2 * blk)
                    bias_ref[0, g, rs, cs] = base
                    bias_ref[1, g, rs, cs] = jnp.where(kj < blk, NEG_INF, base)

    first = (i % nb == 0).astype(jnp.int32)
    lane = lax.broadcasted_iota(jnp.int32, (2 * blk, LANES), 1)
    lo = lane < HEAD_DIM
    lo_q = lax.broadcasted_iota(jnp.int32, (blk, LANES), 1) < HEAD_DIM

    def blockdiag(prev_ref, cur_ref, g, scale):
        c, half = g // 2, g % 2
        cs = slice(c * LANES, (c + 1) * LANES)
        t2 = jnp.concatenate([prev_ref[:, cs], cur_ref[:, cs]], axis=0).astype(F32) * scale
        rolled = pltpu.roll(t2, HEAD_DIM, 1)
        if half == 0:
            a = jnp.where(lo, t2, 0.0)
            b = jnp.where(lo, 0.0, rolled)
        else:
            a = jnp.where(lo, rolled, 0.0)
            b = jnp.where(lo, 0.0, t2)
        return jnp.concatenate([a, b], axis=0).astype(BF16)

    for g in range(n_kv):
        kbd = blockdiag(kp_ref, kc_ref, g, HEAD_DIM ** -0.5)
        vbd = blockdiag(vp_ref, vc_ref, g, 1.0)
        qg = jnp.concatenate(
            [q_ref[:, (g * pairs + p) * LANES:(g * pairs + p + 1) * LANES] for p in range(pairs)], axis=0)
        s = lax.dot_general(qg, kbd, (((1,), (1,)), ((), ())), preferred_element_type=F32)
        s = s + bias_ref[first, g]
        inv = []
        for p in range(pairs):
            for j in range(2):
                rs = slice(p * blk, (p + 1) * blk)
                cs = slice(j * 2 * blk, (j + 1) * 2 * blk)
                sub = s[rs, cs]
                sink = sink_ref[g * Q_PER_KV + 2 * p + j]
                m = jnp.maximum(jnp.max(sub, axis=-1, keepdims=True), sink)
                e = jnp.exp(sub - m)
                denom = jnp.sum(e, axis=-1, keepdims=True) + jnp.exp(sink - m)
                p_ref[rs, cs] = e.astype(BF16)
                inv.append(1.0 / denom)
        o = jnp.dot(p_ref[...], vbd, preferred_element_type=F32)
        for p in range(pairs):
            scale = jnp.where(lo_q, inv[2 * p], inv[2 * p + 1])
            col = (g * pairs + p) * LANES
            o_ref[:, col:col + LANES] = (o[p * blk:(p + 1) * blk] * scale).astype(o_ref.dtype)


def _attention(qkv, sinks, seq):
    t, w = qkv.shape
    blk = ATT_BLOCK
    nb = seq // blk
    n_kv = (w // HEAD_DIM) // (Q_PER_KV + 2)
    dq = n_kv * Q_PER_KV * HEAD_DIM
    dkv = n_kv * HEAD_DIM
    kcol = dq // dkv
    pairs = Q_PER_KV // 2
    slopes = _alibi_slopes(n_kv * Q_PER_KV)

    def prev(i):
        return jnp.where(i % nb == 0, i, i - 1)

    return pl.pallas_call(
        functools.partial(_attn_kernel, nb=nb, n_kv=n_kv, slopes=slopes),
        grid=(t // blk,),
        in_specs=[
            pl.BlockSpec(memory_space=pltpu.SMEM),
            pl.BlockSpec((blk, dq), lambda i: (i, 0)),
            pl.BlockSpec((blk, dkv), lambda i: (i, kcol)),
            pl.BlockSpec((blk, dkv), lambda i: (i, kcol + 1)),
            pl.BlockSpec((blk, dkv), lambda i: (prev(i), kcol)),
            pl.BlockSpec((blk, dkv), lambda i: (prev(i), kcol + 1)),
        ],
        out_specs=pl.BlockSpec((blk, dq), lambda i: (i, 0)),
        out_shape=jax.ShapeDtypeStruct((t, dq), BF16),
        scratch_shapes=[
            pltpu.VMEM((2, n_kv, pairs * blk, 4 * blk), F32),
            pltpu.VMEM((pairs * blk, 4 * blk), BF16),
        ],
        compiler_params=_cp(("arbitrary",)),
        name="swa_attention",
    )(sinks, qkv, qkv, qkv, qkv, qkv)


SUBLANES = 8


def _rows_from_tiles(x3):
    xt = jnp.transpose(x3, (1, 0, 2))
    return jnp.concatenate([xt[s] for s in range(x3.shape[1])], axis=1)


def _tiles_from_rows(x2):
    xt = jnp.stack([x2[:, s * LANES:(s + 1) * LANES] for s in range(x2.shape[1] // LANES)], axis=0)
    return jnp.transpose(xt, (1, 0, 2))


def _pack_rows(h):
    half = h.shape[1] // 2
    bits = lax.bitcast_convert_type(h.astype(BF16).astype(F32), jnp.uint32)
    return (bits[:, :half] >> 16) | (bits[:, half:] & jnp.uint32(0xFFFF0000))


def _unpack_f32(p):
    lo = lax.bitcast_convert_type(p << 16, F32)
    hi = lax.bitcast_convert_type(p & jnp.uint32(0xFFFF0000), F32)
    return lo, hi


def _unpack_rows(p):
    lo, hi = _unpack_f32(p)
    return lo.astype(BF16), hi.astype(BF16)


def _swiglu_packed(p, wg, wu, wd):
    lo, hi = _unpack_rows(p)
    half = p.shape[1]
    gt = (jnp.dot(lo, wg[:half], preferred_element_type=F32)
          + jnp.dot(hi, wg[half:], preferred_element_type=F32))
    up = (jnp.dot(lo, wu[:half], preferred_element_type=F32)
          + jnp.dot(hi, wu[half:], preferred_element_type=F32))
    act = (gt * jax.nn.sigmoid(gt) * up).astype(BF16)
    return jnp.dot(act, wd, preferred_element_type=F32)


def _resid_ln(y, x, mod_ref, lng_ref, lnb_ref):
    r = DEEPNORM_ALPHA * x + (1.0 + mod_ref[2:3, :]) * y
    return _layer_norm(r, lng_ref[...], lnb_ref[...])


def _proj_kernel(a_ref, w_ref, b_ref, x_ref, mod_ref, nmod_ref, lng_ref, lnb_ref, wrh_ref, wrl_ref,
                 xo_ref, hp_ref, lt_ref, acc_ref, *, nk, slab):
    k = pl.program_id(1)

    @pl.when(k == 0)
    def _zero():
        acc_ref[...] = jnp.zeros_like(acc_ref)

    acc_ref[...] += jnp.dot(a_ref[...], w_ref[...], preferred_element_type=F32)

    @pl.when(k == nk - 1)
    def _finish():
        for r0 in range(0, acc_ref.shape[0], slab):
            rs = slice(r0, r0 + slab)
            xn = _resid_ln(acc_ref[rs, :] + b_ref[...], x_ref[rs, :], mod_ref, lng_ref, lnb_ref)
            xo_ref[rs, :] = xn
            h = xn * (1.0 + nmod_ref[1:2, :]) + nmod_ref[0:1, :]
            hp_ref[rs] = _tiles_from_rows(_pack_rows(h))
            h_hi = h.astype(BF16)
            h_lo = (h - h_hi.astype(F32)).astype(BF16)
            lg = (jnp.dot(h_hi, wrh_ref[...], preferred_element_type=F32)
                  + (jnp.dot(h_hi, wrl_ref[...], preferred_element_type=F32)
                     + jnp.dot(h_lo, wrh_ref[...], preferred_element_type=F32)))
            lt_ref[:, rs] = jnp.transpose(lg)[:lt_ref.shape[0]]


def _proj_resid_ln(a, w, b, x2, mods, sub, ln_g, ln_b, w_router, seq):
    t, kdim = a.shape
    d = w.shape[1]
    tm = 512
    tk = min(kdim, 1024)
    nk = kdim // tk
    per_b = seq // tm
    ne = w_router.shape[1]
    wr = jnp.zeros((d, LANES), F32).at[:, :ne].set(w_router)
    wr_hi = wr.astype(BF16)
    wr_lo = (wr - wr_hi.astype(F32)).astype(BF16)
    row = lambda i, k: (i, 0)
    const = lambda i, k: (0, 0)
    return pl.pallas_call(
        functools.partial(_proj_kernel, nk=nk, slab=128),
        grid=(t // tm, nk),
        in_specs=[
            pl.BlockSpec((tm, tk), lambda i, k: (i, k)),
            pl.BlockSpec((tk, d), lambda i, k: (k, 0)),
            pl.BlockSpec((1, d), const),
            pl.BlockSpec((tm, d), row),
            pl.BlockSpec((None, None, 3, d), lambda i, k: (sub, i // per_b, 0, 0)),
            pl.BlockSpec((None, None, 3, d), lambda i, k: (sub + 1, i // per_b, 0, 0)),
            pl.BlockSpec((1, d), const),
            pl.BlockSpec((1, d), const),
            pl.BlockSpec((d, LANES), const),
            pl.BlockSpec((d, LANES), const),
        ],
        out_specs=[
            pl.BlockSpec((tm, d), row),
            pl.BlockSpec((tm, SUBLANES, LANES), lambda i, k: (i, 0, 0)),
            pl.BlockSpec((ne, tm), lambda i, k: (0, i)),
        ],
        out_shape=[
            jax.ShapeDtypeStruct((t, d), F32),
            jax.ShapeDtypeStruct((t, SUBLANES, LANES), jnp.uint32),
            jax.ShapeDtypeStruct((ne, t), F32),
        ],
        scratch_shapes=[pltpu.VMEM((tm, d), F32)],
        compiler_params=_cp(("arbitrary", "arbitrary")),
        name="proj_resid_ln",
    )(a, w, b.reshape(1, d), x2, mods, mods, ln_g.reshape(1, d), ln_b.reshape(1, d), wr_hi, wr_lo)


def _route_kernel(lg_ref, bias_ref, eidx_ref, rank_ref, w_ref, cnt_ref, tri_ref, carry_ref):
    i = pl.program_id(0)
    ne, tt = lg_ref.shape
    gsz = ne // N_EXPERT_GROUPS
    neg = -jnp.inf

    @pl.when(i == 0)
    def _init():
        r = lax.broadcasted_iota(jnp.int32, (tt, tt), 0)
        c = lax.broadcasted_iota(jnp.int32, (tt, tt), 1)
        tri_ref[...] = jnp.where(r < c, 1.0, 0.0).astype(BF16)
        carry_ref[...] = jnp.zeros_like(carry_ref)

    def amax(v, ids, limit):
        m = jnp.max(v, axis=0, keepdims=True)
        idx = jnp.min(jnp.where(v == m, ids, limit), axis=0, keepdims=True)
        return m, idx

    s = jax.nn.sigmoid(lg_ref[...])
    sel = s + bias_ref[:, 0:1]

    ids_g = lax.broadcasted_iota(jnp.int32, (gsz, tt), 0).astype(F32)
    gscore = []
    for g in range(N_EXPERT_GROUPS):
        xg = sel[g * gsz:(g + 1) * gsz]
        m1, i1 = amax(xg, ids_g, float(gsz))
        m2 = jnp.max(jnp.where(ids_g == i1, neg, xg), axis=0, keepdims=True)
        gscore.append(m1 + m2)
    work = jnp.concatenate(gscore, axis=0)
    ids_n = lax.broadcasted_iota(jnp.int32, (N_EXPERT_GROUPS, tt), 0).astype(F32)
    gsel = jnp.zeros((N_EXPERT_GROUPS, tt), F32)
    for _ in range(TOPK_GROUPS):
        _, ig = amax(work, ids_n, float(N_EXPERT_GROUPS))
        pick = ids_n == ig
        gsel = jnp.where(pick, 1.0, gsel)
        work = jnp.where(pick, neg, work)
    work = jnp.concatenate(
        [jnp.where(gsel[g:g + 1] > 0.5, sel[g * gsz:(g + 1) * gsz], NEG_INF)
         for g in range(N_EXPERT_GROUPS)], axis=0)

    ids_e = lax.broadcasted_iota(jnp.int32, (ne, tt), 0).astype(F32)
    chosen = jnp.zeros((ne, tt), F32)
    picked, svals = [], []
    for _ in range(TOP_K):
        _, ik = amax(work, ids_e, float(ne))
        pick = ids_e == ik
        chosen = jnp.where(pick, 1.0, chosen)
        work = jnp.where(pick, neg, work)
        picked.append(ik)
        svals.append(jnp.sum(jnp.where(pick, s, 0.0), axis=0, keepdims=True))
    denom = svals[0]
    for k in range(1, TOP_K):
        denom = denom + svals[k]

    prefix = jnp.dot(chosen.astype(BF16), tri_ref[...], preferred_element_type=F32)
    rank_dense = prefix + carry_ref[:, 0:1]
    carry_ref[...] = carry_ref[...] + jnp.sum(chosen, axis=1, keepdims=True)
    cnt_ref[...] = carry_ref[...]
    for k in range(TOP_K):
        pick = ids_e == picked[k]
        eidx_ref[k:k + 1, :] = picked[k].astype(jnp.int32)
        rank_ref[k:k + 1, :] = jnp.sum(jnp.where(pick, rank_dense, 0.0), axis=0,
                                       keepdims=True).astype(jnp.int32)
        w_ref[k:k + 1, :] = svals[k] / denom * ROUTED_SCALE


def _route(logits_t, router_bias):
    ne, t = logits_t.shape
    tt = 1024
    bias = jnp.broadcast_to(router_bias.astype(F32)[:, None], (ne, LANES))
    tok = lambda i: (0, i)
    const = lambda i: (0, 0)
    return pl.pallas_call(
        _route_kernel,
        grid=(t // tt,),
        in_specs=[pl.BlockSpec((ne, tt), tok), pl.BlockSpec((ne, LANES), const)],
        out_specs=[
            pl.BlockSpec((TOP_K, tt), tok),
            pl.BlockSpec((TOP_K, tt), tok),
            pl.BlockSpec((TOP_K, tt), tok),
            pl.BlockSpec((ne, LANES), const),
        ],
        out_shape=[
            jax.ShapeDtypeStruct((TOP_K, t), jnp.int32),
            jax.ShapeDtypeStruct((TOP_K, t), jnp.int32),
            jax.ShapeDtypeStruct((TOP_K, t), F32),
            jax.ShapeDtypeStruct((ne, LANES), F32),
        ],
        scratch_shapes=[pltpu.VMEM((tt, tt), BF16), pltpu.VMEM((ne, LANES), F32)],
        compiler_params=_cp(("arbitrary",)),
        name="moe_route",
    )(logits_t, bias)


def _dest_kernel(pstart_ref, eidx_ref, rank_ref, dest_ref):
    e = eidx_ref[...]
    base = jnp.zeros_like(e)
    for x in range(N_EXPERTS):
        base = jnp.where(e == x, pstart_ref[x], base)
    dest_ref[...] = rank_ref[...] + base


def _dest_rows(pstart, eidx, rank):
    return pl.pallas_call(
        _dest_kernel,
        in_specs=[pl.BlockSpec(memory_space=pltpu.SMEM),
                  pl.BlockSpec(memory_space=pltpu.VMEM),
                  pl.BlockSpec(memory_space=pltpu.VMEM)],
        out_specs=pl.BlockSpec(memory_space=pltpu.VMEM),
        out_shape=jax.ShapeDtypeStruct(eidx.shape, jnp.int32),
        name="moe_dest",
    )(pstart, eidx, rank)


def _dispatch_kernel(pend_ref, has_ref, dest_ref, hp_ref, wg_ref, wu_ref, wd_ref, xs_hbm, sh_ref,
                     zbuf, sem, zsem):
    i = pl.program_id(0)
    td = hp_ref.shape[0]
    bm = zbuf.shape[0]

    @pl.when(i == 0)
    def _zero_tails():
        zbuf[...] = jnp.zeros_like(zbuf)

        def tail(e):
            start_row = pl.multiple_of(pend_ref[e] - bm, bm)
            return pltpu.make_async_copy(zbuf, xs_hbm.at[pl.ds(start_row, bm)], zsem)

        def start(e, carry):
            @pl.when(has_ref[e] == 1)
            def _():
                tail(e).start()
            return carry

        def wait(e, carry):
            @pl.when(has_ref[e] == 1)
            def _():
                tail(e).wait()
            return carry

        lax.fori_loop(0, N_EXPERTS, start, 0)
        lax.fori_loop(0, N_EXPERTS, wait, 0)

    def row(tok, k):
        return pltpu.make_async_copy(hp_ref.at[pl.ds(tok, 1)],
                                     xs_hbm.at[pl.ds(dest_ref[k * td + tok], 1)], sem)

    def start_rows(g, carry):
        for j in range(SUBLANES):
            for k in range(TOP_K):
                row(g * SUBLANES + j, k).start(priority=k % 2)
        return carry

    def wait_rows(g, carry):
        for j in range(SUBLANES):
            for k in range(TOP_K):
                row(g * SUBLANES + j, k).wait()
        return carry

    lax.fori_loop(0, td // SUBLANES, start_rows, 0)
    sh_ref[...] = _swiglu_packed(_rows_from_tiles(hp_ref[...]), wg_ref[...], wu_ref[...], wd_ref[...])
    lax.fori_loop(0, td // SUBLANES, wait_rows, 0)


def _dispatch(hp, dest_tiles, pends, has, n_rows, td, wg, wu, wd):
    t = hp.shape[0]
    tile = hp.shape[1:]
    d, ff = wg.shape
    const = lambda i, pe, ha: (0, 0)
    grid_spec = pltpu.PrefetchScalarGridSpec(
        num_scalar_prefetch=2,
        grid=(t // td,),
        in_specs=[
            pl.BlockSpec((TOP_K * td,), lambda i, pe, ha: (i,), memory_space=pltpu.SMEM),
            pl.BlockSpec((td,) + tile, lambda i, pe, ha: (i, 0, 0)),
            pl.BlockSpec((d, ff), const),
            pl.BlockSpec((d, ff), const),
            pl.BlockSpec((ff, d), const),
        ],
        out_specs=[pl.BlockSpec(memory_space=pl.ANY),
                   pl.BlockSpec((td, d), lambda i, pe, ha: (i, 0))],
        scratch_shapes=[pltpu.VMEM((MOE_BM,) + tile, jnp.uint32), pltpu.SemaphoreType.DMA(()),
                        pltpu.SemaphoreType.DMA(())],
    )
    return pl.pallas_call(
        _dispatch_kernel,
        grid_spec=grid_spec,
        out_shape=[jax.ShapeDtypeStruct((n_rows,) + tile, jnp.uint32),
                   jax.ShapeDtypeStruct((t, d), F32)],
        compiler_params=_cp(("arbitrary",)),
        name="moe_dispatch",
    )(pends, has, dest_tiles, hp, wg, wu, wd)


def _experts_kernel(blk_e_ref, blk_v_ref, blk_src_ref, xs_ref, wg_ref, wu_ref, wd_ref, o_ref,
                    wgb, wub, wdb):
    del blk_src_ref
    b = pl.program_id(0)
    valid = blk_v_ref[b] == 1
    changed = (b == 0) | (blk_e_ref[b] != blk_e_ref[jnp.maximum(b - 1, 0)])

    @pl.when(valid & changed)
    def _cast_weights():
        wgb[...] = wg_ref[...].astype(BF16)
        wub[...] = wu_ref[...].astype(BF16)
        wdb[...] = wd_ref[...].astype(BF16)

    @pl.when(valid)
    def _compute():
        y = _swiglu_packed(_rows_from_tiles(xs_ref[...]), wgb[...], wub[...], wdb[...])
        o_ref[...] = _tiles_from_rows(_pack_rows(y))


def _experts(xs, blk_e, blk_v, blk_src, wg, wu, wd, layer):
    _, ne, d, ff = wg.shape
    bm = MOE_BM
    nblk = blk_e.shape[0]
    tile = xs.shape[1:]
    wmap = lambda b, be, bv, bs: (layer, be[b], 0, 0)
    rows = lambda b, be, bv, bs: (bs[b], 0, 0)
    grid_spec = pltpu.PrefetchScalarGridSpec(
        num_scalar_prefetch=3,
        grid=(nblk,),
        in_specs=[
            pl.BlockSpec((bm,) + tile, rows),
            pl.BlockSpec((None, None, d, ff), wmap),
            pl.BlockSpec((None, None, d, ff), wmap),
            pl.BlockSpec((None, None, ff, d), wmap),
        ],
        out_specs=pl.BlockSpec((bm,) + tile, rows),
        scratch_shapes=[pltpu.VMEM((d, ff), BF16), pltpu.VMEM((d, ff), BF16), pltpu.VMEM((ff, d), BF16)],
    )
    return pl.pallas_call(
        _experts_kernel,
        grid_spec=grid_spec,
        out_shape=jax.ShapeDtypeStruct((nblk * bm,) + tile, jnp.uint32),
        compiler_params=_cp(("arbitrary",)),
        name="moe_experts",
    )(blk_e, blk_v, blk_src, xs, wg, wu, wd)


def _combine_kernel(dcur_ref, dnxt_ref, y_hbm, wts_ref, sh_ref, x_ref, mod_ref, nmod_ref, lng_ref,
                    lnb_ref, *rest, last):
    if last:
        xo_ref, ybuf, sem = rest
    else:
        xo_ref, ho_ref, ybuf, sem = rest
    tt = x_ref.shape[0]
    i = pl.program_id(0)
    slot = i % 2

    def row(dref, slot_, tok, k):
        return pltpu.make_async_copy(y_hbm.at[pl.ds(dref[k * tt + tok], 1)],
                                     ybuf.at[slot_, pl.ds(k * tt + tok, 1)], sem.at[slot_])

    grp = 2 * SUBLANES
    ngrp = tt // grp

    def tile_loop(dref, slot_, op):
        def body(g, carry):
            for j in range(grp):
                for k in range(TOP_K):
                    copy = row(dref, slot_, g * grp + j, k)
                    if op == "start":
                        copy.start(priority=k % 2)
                    else:
                        copy.wait()
            return carry

        lax.fori_loop(0, ngrp, body, 0)

    @pl.when(i == 0)
    def _prime():
        tile_loop(dcur_ref, 0, "start")

    @pl.when(i + 1 < pl.num_programs(0))
    def _prefetch():
        tile_loop(dnxt_ref, 1 - slot, "start")

    tile_loop(dcur_ref, slot, "wait")

    wb = [jnp.broadcast_to(wts_ref[:, k:k + 1], (tt, LANES)) for k in range(TOP_K)]
    lo_cols = [None] * SUBLANES
    hi_cols = [None] * SUBLANES
    for k in range(TOP_K):
        yt = jnp.transpose(ybuf[slot, pl.ds(k * tt, tt)], (1, 0, 2))
        for s in range(SUBLANES):
            lo, hi = _unpack_f32(yt[s])
            lo_cols[s] = lo * wb[k] if k == 0 else lo_cols[s] + lo * wb[k]
            hi_cols[s] = hi * wb[k] if k == 0 else hi_cols[s] + hi * wb[k]
    routed = jnp.concatenate(lo_cols + hi_cols, axis=1)
    xn = _resid_ln(routed + sh_ref[...], x_ref[...], mod_ref, lng_ref, lnb_ref)
    xo_ref[...] = xn
    if not last:
        ho_ref[...] = (xn * (1.0 + nmod_ref[1:2, :]) + nmod_ref[0:1, :]).astype(ho_ref.dtype)


def _combine(ys, dest_tiles, wts, shared, x2, mods, sub, ln_g, ln_b, seq, last):
    t, d = x2.shape
    tt = COMB_TT
    nt = t // tt
    per_b = seq // tt
    nsub = sub if last else sub + 1
    row = lambda i: (i, 0)
    const = lambda i: (0, 0)
    outs = pl.pallas_call(
        functools.partial(_combine_kernel, last=last),
        grid=(nt,),
        in_specs=[
            pl.BlockSpec((tt * TOP_K,), lambda i: (i,), memory_space=pltpu.SMEM),
            pl.BlockSpec((tt * TOP_K,), lambda i: (jnp.minimum(i + 1, nt - 1),), memory_space=pltpu.SMEM),
            pl.BlockSpec(memory_space=pl.ANY),
            pl.BlockSpec((tt, TOP_K), row),
            pl.BlockSpec((tt, d), row),
            pl.BlockSpec((tt, d), row),
            pl.BlockSpec((None, None, 3, d), lambda i: (sub, i // per_b, 0, 0)),
            pl.BlockSpec((None, None, 3, d), lambda i: (nsub, i // per_b, 0, 0)),
            pl.BlockSpec((1, d), const),
            pl.BlockSpec((1, d), const),
        ],
        out_specs=[pl.BlockSpec((tt, d), row)] * (1 if last else 2),
        out_shape=[jax.ShapeDtypeStruct((t, d), F32)] + ([] if last else [jax.ShapeDtypeStruct((t, d), BF16)]),
        scratch_shapes=[pltpu.VMEM((2, TOP_K * tt) + ys.shape[1:], jnp.uint32),
                        pltpu.SemaphoreType.DMA((2,))],
        compiler_params=_cp(("arbitrary",)),
        name="moe_combine",
    )(dest_tiles, dest_tiles, ys, wts, shared, x2, mods, mods, ln_g.reshape(1, d), ln_b.reshape(1, d))
    return (outs[0], None) if last else (outs[0], outs[1])


def _tile_major(a, tile):
    k, t = a.shape
    return a.reshape(k, t // tile, tile).transpose(1, 0, 2).reshape(-1)


def _moe(hp, logits_t, x2, mods, sub, p, layer, seq, last):
    t = hp.shape[0]
    bm = MOE_BM
    eidx, rank, w_t, cnt = _route(logits_t, p["router_bias"])
    counts = cnt[:, 0].astype(jnp.int32)
    padded = (counts + bm - 1) // bm * bm
    pends = jnp.cumsum(padded)
    pstart = pends - padded
    nblk = (t * TOP_K) // bm + N_EXPERTS
    blk_ids = jnp.arange(nblk, dtype=jnp.int32)
    blk_v = (blk_ids * bm < pends[-1]).astype(jnp.int32)
    blk_src = jnp.minimum(blk_ids, pends[-1] // bm - 1).astype(jnp.int32)
    blk_e = jnp.sum((blk_src * bm)[:, None] >= pends[None, :], axis=1).astype(jnp.int32)
    has = (counts > 0).astype(jnp.int32)

    dest = _dest_rows(pstart.astype(jnp.int32), eidx, rank)
    xs, sh = _dispatch(hp, _tile_major(dest, DISP_TD), pends.astype(jnp.int32), has, nblk * bm, DISP_TD,
                       p["ws_gate"].astype(BF16), p["ws_up"].astype(BF16), p["ws_down"].astype(BF16))
    ys = _experts(xs, blk_e, blk_v, blk_src, p["w_gate"], p["w_up"], p["w_down"], layer)
    return _combine(ys, _tile_major(dest, COMB_TT), jnp.transpose(w_t), sh, x2, mods, sub,
                    p["ln_g"], p["ln_b"], seq, last)


def kernel(x, c, ada_w, ada_b, ln_g, ln_b, gm_w_in, gm_b_in, gm_v_ln_g, gm_v_ln_b, gm_w_s, gm_b_s,
           gm_w_out, gm_b_out, at_w_qkv, at_b_qkv, at_sinks, at_w_o, at_b_o, moe_w_router,
           moe_router_bias, moe_w_gate, moe_w_up, moe_w_down, moe_ws_gate, moe_ws_up, moe_ws_down):
    bsz, seq, d = x.shape
    t = bsz * seq
    depth = ada_w.shape[0]
    x2 = x.reshape(t, d)

    rows = 8
    c_pad = jnp.zeros((rows, d), F32).at[:bsz].set(c)
    mods = _adamod(c_pad, ada_w.reshape(depth * 2, d, 3 * d), ada_b.reshape(depth * 2, 3 * d))
    mods = mods.reshape(depth * 2, rows, 3, d)

    h = _modulate(x2, mods, 0, seq)
    for i in range(depth):
        j = i // 2
        wrt = moe_w_router[i]
        if i % 2 == 0:
            z = _mm(h, gm_w_in, j, gm_b_in[j], act="gelu", tm=512, tn=1024, out_dtype=BF16)
            a = _gmlp_gate(z, gm_v_ln_g[j], gm_v_ln_b[j], gm_w_s[j], gm_b_s[j])
            x2, hp, logits_t = _proj_resid_ln(a, gm_w_out[j].astype(BF16), gm_b_out[j], x2, mods, 2 * i,
                                              ln_g[i, 0], ln_b[i, 0], wrt, seq)
        else:
            qkv = _mm(h, at_w_qkv, j, at_b_qkv[j], act=None, tm=512, tn=1280, out_dtype=BF16)
            o = _attention(qkv, at_sinks[j], seq)
            x2, hp, logits_t = _proj_resid_ln(o, at_w_o[j].astype(BF16), at_b_o[j], x2, mods, 2 * i,
                                              ln_g[i, 0], ln_b[i, 0], wrt, seq)
        p = dict(router_bias=moe_router_bias[i], w_gate=moe_w_gate, w_up=moe_w_up, w_down=moe_w_down,
                 ws_gate=moe_ws_gate[i], ws_up=moe_ws_up[i], ws_down=moe_ws_down[i],
                 ln_g=ln_g[i, 1], ln_b=ln_b[i, 1])
        x2, h = _moe(hp, logits_t, x2, mods, 2 * i + 1, p, i, seq, last=(i == depth - 1))
    return x2.reshape(bsz, seq, d)
```

```python
import functools
import math

import numpy as np
import jax
import jax.numpy as jnp
from jax import lax
from jax.experimental import pallas as pl
from jax.experimental.pallas import tpu as pltpu

F32 = jnp.float32
BF16 = jnp.bfloat16

DEPTH = 4
CHUNK = 128
GMLP_GROUPS = 8
HEAD_DIM = 64
Q_PER_KV = 8
ATT_BLOCK = 128
N_EXPERTS = 64
TOP_K = 8
N_EXPERT_GROUPS = 8
TOPK_GROUPS = 4
ROUTED_SCALE = 2.5
DEEPNORM_ALPHA = (2.0 * DEPTH) ** 0.25
LN_EPS = 1e-5
NEG_INF = -1e30

LANES = 128
VMEM_LIMIT = 56 * 1024 * 1024

MOE_BM = 512
COMB_TT = 128
DISP_TD = 256


def _cp(sem, vmem=VMEM_LIMIT):
    return pltpu.CompilerParams(dimension_semantics=sem, vmem_limit_bytes=vmem)


def _layer_norm(x, g, b):
    mu = jnp.mean(x, axis=-1, keepdims=True)
    xc = x - mu
    var = jnp.mean(xc * xc, axis=-1, keepdims=True)
    return xc * lax.rsqrt(var + LN_EPS) * g + b


def _alibi_slopes(n):
    p = 2 ** int(math.floor(math.log2(n)))
    base = [2.0 ** (-8.0 * (i + 1) / p) for i in range(p)]
    extra = [2.0 ** (-4.0 * (2 * i + 1) / p) for i in range(n - p)]
    return [float(np.float32(s)) for s in base + extra]


def _adamod_kernel(c_ref, w_ref, b_ref, o_ref):
    c = c_ref[...]
    ca = (c * jax.nn.sigmoid(c)).astype(BF16)
    o_ref[...] = jnp.dot(ca, w_ref[...].astype(BF16), preferred_element_type=F32) + b_ref[...]


def _adamod(c_pad, ada_w, ada_b):
    ns, d, d3 = ada_w.shape
    rows = c_pad.shape[0]
    tn = 768
    return pl.pallas_call(
        _adamod_kernel,
        grid=(ns, d3 // tn),
        in_specs=[
            pl.BlockSpec((rows, d), lambda s, j: (0, 0)),
            pl.BlockSpec((None, d, tn), lambda s, j: (s, 0, j)),
            pl.BlockSpec((None, 1, tn), lambda s, j: (s, 0, j)),
        ],
        out_specs=pl.BlockSpec((None, rows, tn), lambda s, j: (s, 0, j)),
        out_shape=jax.ShapeDtypeStruct((ns, rows, d3), F32),
        compiler_params=_cp(("arbitrary", "arbitrary")),
        name="adamod",
    )(c_pad, ada_w, ada_b.reshape(ns, 1, d3))


def _modulate_kernel(x_ref, mod_ref, h_ref):
    h = x_ref[...] * (1.0 + mod_ref[1:2, :]) + mod_ref[0:1, :]
    h_ref[...] = h.astype(h_ref.dtype)


def _modulate(x2, mods, sub, seq):
    t, d = x2.shape
    tm = 512
    per_b = seq // tm
    return pl.pallas_call(
        _modulate_kernel,
        grid=(t // tm,),
        in_specs=[
            pl.BlockSpec((tm, d), lambda i: (i, 0)),
            pl.BlockSpec((None, None, 3, d), lambda i: (sub, i // per_b, 0, 0)),
        ],
        out_specs=pl.BlockSpec((tm, d), lambda i: (i, 0)),
        out_shape=jax.ShapeDtypeStruct((t, d), BF16),
        compiler_params=_cp(("arbitrary",)),
        name="modulate",
    )(x2, mods)


def _mm_kernel(a_ref, w_ref, b_ref, o_ref, wb_ref, *, act):
    @pl.when(pl.program_id(1) == 0)
    def _cast_weights():
        wb_ref[...] = w_ref[...].astype(BF16)

    acc = jnp.dot(a_ref[...], wb_ref[...], preferred_element_type=F32) + b_ref[...]
    if act == "gelu":
        acc = 0.5 * acc * (1.0 + lax.erf(acc * np.float32(math.sqrt(0.5))))
    o_ref[...] = acc.astype(o_ref.dtype)


def _mm(a, w_stack, layer, b, *, act, tm, tn, out_dtype):
    m, k = a.shape
    n = w_stack.shape[2]
    return pl.pallas_call(
        functools.partial(_mm_kernel, act=act),
        grid=(n // tn, m // tm),
        in_specs=[
            pl.BlockSpec((tm, k), lambda j, i: (i, 0)),
            pl.BlockSpec((None, k, tn), lambda j, i: (layer, 0, j)),
            pl.BlockSpec((1, tn), lambda j, i: (0, j)),
        ],
        out_specs=pl.BlockSpec((tm, tn), lambda j, i: (i, j)),
        out_shape=jax.ShapeDtypeStruct((m, n), out_dtype),
        scratch_shapes=[pltpu.VMEM((k, tn), BF16)],
        compiler_params=_cp(("arbitrary", "arbitrary")),
        name="mm_" + (act or "bias"),
    )(a, w_stack, b.reshape(1, n))


def _attn_kernel(sink_ref, q_ref, kc_ref, vc_ref, kp_ref, vp_ref, o_ref, bias_ref, p_ref,
                 *, nb, n_kv, slopes):
    blk = ATT_BLOCK
    pairs = Q_PER_KV // 2
    i = pl.program_id(0)

    @pl.when(i == 0)
    def _build_bias():
        qi = lax.broadcasted_iota(jnp.int32, (blk, 2 * blk), 0)
        kj = lax.broadcasted_iota(jnp.int32, (blk, 2 * blk), 1)
        dist = qi + blk - kj
        valid = (dist >= 0) & (dist < blk)
        distf = dist.astype(F32)
        for g in range(n_kv):
            for p in range(pairs):
                for j in range(2):
                    slope = slopes[g * Q_PER_KV + 2 * p + j]
                    base = jnp.where(valid, -slope * distf, NEG_INF)
                    rs = slice(p * blk, (p + 1) * blk)
                    cs = slice(j * 2 * blk, (j + 1) * 2 * blk)
                    bias_ref[0, g, rs, cs] = base
                    bias_ref[1, g, rs, cs] = jnp.where(kj < blk, NEG_INF, base)

    first = (i % nb == 0).astype(jnp.int32)
    lane = lax.broadcasted_iota(jnp.int32, (2 * blk, LANES), 1)
    lo = lane < HEAD_DIM
    lo_q = lax.broadcasted_iota(jnp.int32, (blk, LANES), 1) < HEAD_DIM

    def blockdiag(prev_ref, cur_ref, g, scale):
        c, half = g // 2, g % 2
        cs = slice(c * LANES, (c + 1) * LANES)
        t2 = jnp.concatenate([prev_ref[:, cs], cur_ref[:, cs]], axis=0).astype(F32) * scale
        rolled = pltpu.roll(t2, HEAD_DIM, 1)
        if half == 0:
            a = jnp.where(lo, t2, 0.0)
            b = jnp.where(lo, 0.0, rolled)
        else:
            a = jnp.where(lo, rolled, 0.0)
            b = jnp.where(lo, 0.0, t2)
        return jnp.concatenate([a, b], axis=0).astype(BF16)

    for g in range(n_kv):
        kbd = blockdiag(kp_ref, kc_ref, g, HEAD_DIM ** -0.5)
        vbd = blockdiag(vp_ref, vc_ref, g, 1.0)
        qg = jnp.concatenate(
            [q_ref[:, (g * pairs + p) * LANES:(g * pairs + p + 1) * LANES] for p in range(pairs)], axis=0)
        s = lax.dot_general(qg, kbd, (((1,), (1,)), ((), ())), preferred_element_type=F32)
        s = s + bias_ref[first, g]
        inv = []
        for p in range(pairs):
            for j in range(2):
                rs = slice(p * blk, (p + 1) * blk)
                cs = slice(j * 2 * blk, (j + 1) * 2 * blk)
                sub = s[rs, cs]
                sink = sink_ref[g * Q_PER_KV + 2 * p + j]
                m = jnp.maximum(jnp.max(sub, axis=-1, keepdims=True), sink)
                e = jnp.exp(sub - m)
                denom = jnp.sum(e, axis=-1, keepdims=True) + jnp.exp(sink - m)
                p_ref[rs, cs] = e.astype(BF16)
                inv.append(1.0 / denom)
        o = jnp.dot(p_ref[...], vbd, preferred_element_type=F32)
        for p in range(pairs):
            scale = jnp.where(lo_q, inv[2 * p], inv[2 * p + 1])
            col = (g * pairs + p) * LANES
            o_ref[:, col:col + LANES] = (o[p * blk:(p + 1) * blk] * scale).astype(o_ref.dtype)


def _attention(qkv, sinks, seq):
    t, w = qkv.shape
    blk = ATT_BLOCK
    nb = seq // blk
    n_kv = (w // HEAD_DIM) // (Q_PER_KV + 2)
    dq = n_kv * Q_PER_KV * HEAD_DIM
    dkv = n_kv * HEAD_DIM
    kcol = dq // dkv
    pairs = Q_PER_KV // 2
    slopes = _alibi_slopes(n_kv * Q_PER_KV)

    def prev(i):
        return jnp.where(i % nb == 0, i, i - 1)

    return pl.pallas_call(
        functools.partial(_attn_kernel, nb=nb, n_kv=n_kv, slopes=slopes),
        grid=(t // blk,),
        in_specs=[
            pl.BlockSpec(memory_space=pltpu.SMEM),
            pl.BlockSpec((blk, dq), lambda i: (i, 0)),
            pl.BlockSpec((blk, dkv), lambda i: (i, kcol)),
            pl.BlockSpec((blk, dkv), lambda i: (i, kcol + 1)),
            pl.BlockSpec((blk, dkv), lambda i: (prev(i), kcol)),
            pl.BlockSpec((blk, dkv), lambda i: (prev(i), kcol + 1)),
        ],
        out_specs=pl.BlockSpec((blk, dq), lambda i: (i, 0)),
        out_shape=jax.ShapeDtypeStruct((t, dq), BF16),
        scratch_shapes=[
            pltpu.VMEM((2, n_kv, pairs * blk, 4 * blk), F32),
            pltpu.VMEM((pairs * blk, 4 * blk), BF16),
        ],
        compiler_params=_cp(("arbitrary",)),
        name="swa_attention",
    )(sinks, qkv, qkv, qkv, qkv, qkv)


SUBLANES = 8


def _rows_from_tiles(x3):
    xt = jnp.transpose(x3, (1, 0, 2))
    return jnp.concatenate([xt[s] for s in range(x3.shape[1])], axis=1)


def _tiles_from_rows(x2):
    xt = jnp.stack([x2[:, s * LANES:(s + 1) * LANES] for s in range(x2.shape[1] // LANES)], axis=0)
    return jnp.transpose(xt, (1, 0, 2))


def _pack_rows(h):
    half = h.shape[1] // 2
    bits = lax.bitcast_convert_type(h.astype(BF16).astype(F32), jnp.uint32)
    return (bits[:, :half] >> 16) | (bits[:, half:] & jnp.uint32(0xFFFF0000))


def _unpack_f32(p):
    lo = lax.bitcast_convert_type(p << 16, F32)
    hi = lax.bitcast_convert_type(p & jnp.uint32(0xFFFF0000), F32)
    return lo, hi


def _unpack_rows(p):
    lo, hi = _unpack_f32(p)
    return lo.astype(BF16), hi.astype(BF16)


def _swiglu_packed(p, wg, wu, wd):
    lo, hi = _unpack_rows(p)
    half = p.shape[1]
    gt = (jnp.dot(lo, wg[:half], preferred_element_type=F32)
          + jnp.dot(hi, wg[half:], preferred_element_type=F32))
    up = (jnp.dot(lo, wu[:half], preferred_element_type=F32)
          + jnp.dot(hi, wu[half:], preferred_element_type=F32))
    act = (gt * jax.nn.sigmoid(gt) * up).astype(BF16)
    return jnp.dot(act, wd, preferred_element_type=F32)


def _resid_ln(y, x, mod_ref, lng_ref, lnb_ref):
    r = DEEPNORM_ALPHA * x + (1.0 + mod_ref[2:3, :]) * y
    return _layer_norm(r, lng_ref[...], lnb_ref[...])


def _proj_kernel(a_ref, w_ref, *rest, nk, slab):
    k = pl.program_id(1)
    acc_ref = rest[-1]

    @pl.when(k == 0)
    def _zero():
        acc_ref[...] = jnp.zeros_like(acc_ref)

    acc_ref[...] += jnp.dot(a_ref[...], w_ref[...], preferred_element_type=F32)
    _proj_finish(k == nk - 1, *rest, slab=slab)


def _gate_proj_kernel(u_ref, v_ref, vall_ref, vg_ref, vb_ref, ws_ref, bs_ref, w_ref, *rest, nk, slab):
    k = pl.program_id(1)
    acc_ref, mu_ref, rstd_ref = rest[-3:]
    tm, gw = u_ref.shape

    @pl.when(k == 0)
    def _start():
        acc_ref[...] = jnp.zeros_like(acc_ref)
        for r0 in range(0, tm, slab):
            rs = slice(r0, r0 + slab)
            v = vall_ref[rs, :].astype(F32)
            mu = jnp.mean(v, axis=-1, keepdims=True)
            var = jnp.mean(v * v, axis=-1, keepdims=True) - mu * mu
            mu_ref[rs, :] = jnp.broadcast_to(mu, (slab, LANES))
            rstd_ref[rs, :] = jnp.broadcast_to(lax.rsqrt(var + LN_EPS), (slab, LANES))

    row = lax.broadcasted_iota(jnp.int32, (CHUNK, CHUNK), 0)
    col = lax.broadcasted_iota(jnp.int32, (CHUNK, CHUNK), 1)
    wsg = jnp.where(row >= col, ws_ref[k], 0.0).astype(BF16)
    bs = bs_ref[k][:, 0:1]
    vn = ((v_ref[...].astype(F32) - mu_ref[:, 0:1]) * rstd_ref[:, 0:1] * vg_ref[...]
          + vb_ref[...]).astype(BF16)
    parts = []
    for c in range(tm // CHUNK):
        rs = slice(c * CHUNK, (c + 1) * CHUNK)
        s = jnp.dot(wsg, vn[rs], preferred_element_type=F32) + bs
        parts.append((u_ref[rs, :].astype(F32) * s).astype(BF16))
    a = jnp.concatenate(parts, axis=0)
    acc_ref[...] += jnp.dot(a, w_ref[...], preferred_element_type=F32)
    _proj_finish(k == nk - 1, *rest[:-2], slab=slab)


def _proj_finish(is_last, b_ref, x_ref, mod_ref, nmod_ref, lng_ref, lnb_ref, wrh_ref, wrl_ref,
                 xo_ref, hp_ref, lt_ref, acc_ref, *, slab):
    @pl.when(is_last)
    def _finish():
        for r0 in range(0, acc_ref.shape[0], slab):
            rs = slice(r0, r0 + slab)
            xn = _resid_ln(acc_ref[rs, :] + b_ref[...], x_ref[rs, :], mod_ref, lng_ref, lnb_ref)
            xo_ref[rs, :] = xn
            h = xn * (1.0 + nmod_ref[1:2, :]) + nmod_ref[0:1, :]
            hp_ref[rs] = _tiles_from_rows(_pack_rows(h))
            h_hi = h.astype(BF16)
            h_lo = (h - h_hi.astype(F32)).astype(BF16)
            lg = (jnp.dot(h_hi, wrh_ref[...], preferred_element_type=F32)
                  + (jnp.dot(h_hi, wrl_ref[...], preferred_element_type=F32)
                     + jnp.dot(h_lo, wrh_ref[...], preferred_element_type=F32)))
            lt_ref[:, rs] = jnp.transpose(lg)[:lt_ref.shape[0]]


def _proj_resid_ln(lhs, w, b, x2, mods, sub, ln_g, ln_b, w_router, seq):
    t, d = x2.shape
    tm = 512
    per_b = seq // tm
    ne = w_router.shape[1]
    wr = jnp.zeros((d, LANES), F32).at[:, :ne].set(w_router)
    wr_hi = wr.astype(BF16)
    wr_lo = (wr - wr_hi.astype(F32)).astype(BF16)
    row = lambda i, k: (i, 0)
    const = lambda i, k: (0, 0)
    const3 = lambda i, k: (0, 0, 0)
    if isinstance(lhs, tuple):
        z, vg, vb, w_s, b_s = lhs
        half = z.shape[1] // 2
        nk = GMLP_GROUPS
        tk = half // nk
        body = _gate_proj_kernel
        lhs_args = (z, z, z, vg.reshape(1, half), vb.reshape(1, half), w_s,
                    jnp.broadcast_to(b_s[:, :, None], b_s.shape + (LANES,)))
        lhs_specs = [
            pl.BlockSpec((tm, tk), lambda i, k: (i, k)),
            pl.BlockSpec((tm, tk), lambda i, k: (i, nk + k)),
            pl.BlockSpec((tm, half), lambda i, k: (i, 1)),
            pl.BlockSpec((1, tk), lambda i, k: (0, k)),
            pl.BlockSpec((1, tk), lambda i, k: (0, k)),
            pl.BlockSpec(w_s.shape, const3),
            pl.BlockSpec(b_s.shape + (LANES,), const3),
        ]
        extra_scratch = [pltpu.VMEM((tm, LANES), F32), pltpu.VMEM((tm, LANES), F32)]
    else:
        kdim = lhs.shape[1]
        tk = min(kdim, 1024)
        nk = kdim // tk
        body = _proj_kernel
        lhs_args = (lhs,)
        lhs_specs = [pl.BlockSpec((tm, tk), lambda i, k: (i, k))]
        extra_scratch = []
    return pl.pallas_call(
        functools.partial(body, nk=nk, slab=128),
        grid=(t // tm, nk),
        in_specs=lhs_specs + [
            pl.BlockSpec((tk, d), lambda i, k: (k, 0)),
            pl.BlockSpec((1, d), const),
            pl.BlockSpec((tm, d), row),
            pl.BlockSpec((None, None, 3, d), lambda i, k: (sub, i // per_b, 0, 0)),
            pl.BlockSpec((None, None, 3, d), lambda i, k: (sub + 1, i // per_b, 0, 0)),
            pl.BlockSpec((1, d), const),
            pl.BlockSpec((1, d), const),
            pl.BlockSpec((d, LANES), const),
            pl.BlockSpec((d, LANES), const),
        ],
        out_specs=[
            pl.BlockSpec((tm, d), row),
            pl.BlockSpec((tm, SUBLANES, LANES), lambda i, k: (i, 0, 0)),
            pl.BlockSpec((ne, tm), lambda i, k: (0, i)),
        ],
        out_shape=[
            jax.ShapeDtypeStruct((t, d), F32),
            jax.ShapeDtypeStruct((t, SUBLANES, LANES), jnp.uint32),
            jax.ShapeDtypeStruct((ne, t), F32),
        ],
        scratch_shapes=[pltpu.VMEM((tm, d), F32)] + extra_scratch,
        compiler_params=_cp(("arbitrary", "arbitrary")),
        name="proj_resid_ln",
    )(*lhs_args, w, b.reshape(1, d), x2, mods, mods, ln_g.reshape(1, d), ln_b.reshape(1, d), wr_hi, wr_lo)


def _route_kernel(lg_ref, bias_ref, eidx_ref, rank_ref, w_ref, cnt_ref, tri_ref, carry_ref):
    i = pl.program_id(0)
    ne, tt = lg_ref.shape
    gsz = ne // N_EXPERT_GROUPS
    neg = -jnp.inf

    @pl.when(i == 0)
    def _init():
        r = lax.broadcasted_iota(jnp.int32, (tt, tt), 0)
        c = lax.broadcasted_iota(jnp.int32, (tt, tt), 1)
        tri_ref[...] = jnp.where(r < c, 1.0, 0.0).astype(BF16)
        carry_ref[...] = jnp.zeros_like(carry_ref)

    def amax(v, ids, limit):
        m = jnp.max(v, axis=0, keepdims=True)
        idx = jnp.min(jnp.where(v == m, ids, limit), axis=0, keepdims=True)
        return m, idx

    s = jax.nn.sigmoid(lg_ref[...])
    sel = s + bias_ref[:, 0:1]

    ids_g = lax.broadcasted_iota(jnp.int32, (gsz, tt), 0).astype(F32)
    gscore = []
    for g in range(N_EXPERT_GROUPS):
        xg = sel[g * gsz:(g + 1) * gsz]
        m1, i1 = amax(xg, ids_g, float(gsz))
        m2 = jnp.max(jnp.where(ids_g == i1, neg, xg), axis=0, keepdims=True)
        gscore.append(m1 + m2)
    work = jnp.concatenate(gscore, axis=0)
    ids_n = lax.broadcasted_iota(jnp.int32, (N_EXPERT_GROUPS, tt), 0).astype(F32)
    gsel = jnp.zeros((N_EXPERT_GROUPS, tt), F32)
    for _ in range(TOPK_GROUPS):
        _, ig = amax(work, ids_n, float(N_EXPERT_GROUPS))
        pick = ids_n == ig
        gsel = jnp.where(pick, 1.0, gsel)
        work = jnp.where(pick, neg, work)
    work = jnp.concatenate(
        [jnp.where(gsel[g:g + 1] > 0.5, sel[g * gsz:(g + 1) * gsz], NEG_INF)
         for g in range(N_EXPERT_GROUPS)], axis=0)

    ids_e = lax.broadcasted_iota(jnp.int32, (ne, tt), 0).astype(F32)
    chosen = jnp.zeros((ne, tt), F32)
    picked, svals = [], []
    for _ in range(TOP_K):
        _, ik = amax(work, ids_e, float(ne))
        pick = ids_e == ik
        chosen = jnp.where(pick, 1.0, chosen)
        work = jnp.where(pick, neg, work)
        picked.append(ik)
        svals.append(jnp.sum(jnp.where(pick, s, 0.0), axis=0, keepdims=True))
    denom = svals[0]
    for k in range(1, TOP_K):
        denom = denom + svals[k]

    prefix = jnp.dot(chosen.astype(BF16), tri_ref[...], preferred_element_type=F32)
    rank_dense = prefix + carry_ref[:, 0:1]
    carry_ref[...] = carry_ref[...] + jnp.sum(chosen, axis=1, keepdims=True)
    cnt_ref[...] = carry_ref[...]
    for k in range(TOP_K):
        pick = ids_e == picked[k]
        eidx_ref[k:k + 1, :] = picked[k].astype(jnp.int32)
        rank_ref[k:k + 1, :] = jnp.sum(jnp.where(pick, rank_dense, 0.0), axis=0,
                                       keepdims=True).astype(jnp.int32)
        w_ref[k:k + 1, :] = svals[k] / denom * ROUTED_SCALE


def _route(logits_t, router_bias):
    ne, t = logits_t.shape
    tt = 1024
    bias = jnp.broadcast_to(router_bias.astype(F32)[:, None], (ne, LANES))
    tok = lambda i: (0, i)
    const = lambda i: (0, 0)
    return pl.pallas_call(
        _route_kernel,
        grid=(t // tt,),
        in_specs=[pl.BlockSpec((ne, tt), tok), pl.BlockSpec((ne, LANES), const)],
        out_specs=[
            pl.BlockSpec((TOP_K, tt), tok),
            pl.BlockSpec((TOP_K, tt), tok),
            pl.BlockSpec((TOP_K, tt), tok),
            pl.BlockSpec((ne, LANES), const),
        ],
        out_shape=[
            jax.ShapeDtypeStruct((TOP_K, t), jnp.int32),
            jax.ShapeDtypeStruct((TOP_K, t), jnp.int32),
            jax.ShapeDtypeStruct((TOP_K, t), F32),
            jax.ShapeDtypeStruct((ne, LANES), F32),
        ],
        scratch_shapes=[pltpu.VMEM((tt, tt), BF16), pltpu.VMEM((ne, LANES), F32)],
        compiler_params=_cp(("arbitrary",)),
        name="moe_route",
    )(logits_t, bias)


def _dest_kernel(pstart_ref, eidx_ref, rank_ref, dest_ref):
    e = eidx_ref[...]
    base = jnp.zeros_like(e)
    for x in range(N_EXPERTS):
        base = jnp.where(e == x, pstart_ref[x], base)
    dest_ref[...] = rank_ref[...] + base


def _dest_rows(pstart, eidx, rank):
    return pl.pallas_call(
        _dest_kernel,
        in_specs=[pl.BlockSpec(memory_space=pltpu.SMEM),
                  pl.BlockSpec(memory_space=pltpu.VMEM),
                  pl.BlockSpec(memory_space=pltpu.VMEM)],
        out_specs=pl.BlockSpec(memory_space=pltpu.VMEM),
        out_shape=jax.ShapeDtypeStruct(eidx.shape, jnp.int32),
        name="moe_dest",
    )(pstart, eidx, rank)


def _dispatch_kernel(pend_ref, has_ref, dest_ref, hp_ref, wg_ref, wu_ref, wd_ref, xs_hbm, sh_ref,
                     zbuf, sem, zsem):
    i = pl.program_id(0)
    td = hp_ref.shape[0]
    bm = zbuf.shape[0]

    @pl.when(i == 0)
    def _zero_tails():
        zbuf[...] = jnp.zeros_like(zbuf)

        def tail(e):
            start_row = pl.multiple_of(pend_ref[e] - bm, bm)
            return pltpu.make_async_copy(zbuf, xs_hbm.at[pl.ds(start_row, bm)], zsem)

        def start(e, carry):
            @pl.when(has_ref[e] == 1)
            def _():
                tail(e).start()
            return carry

        def wait(e, carry):
            @pl.when(has_ref[e] == 1)
            def _():
                tail(e).wait()
            return carry

        lax.fori_loop(0, N_EXPERTS, start, 0)
        lax.fori_loop(0, N_EXPERTS, wait, 0)

    def row(tok, k):
        return pltpu.make_async_copy(hp_ref.at[pl.ds(tok, 1)],
                                     xs_hbm.at[pl.ds(dest_ref[k * td + tok], 1)], sem)

    def start_rows(g, carry):
        for j in range(SUBLANES):
            for k in range(TOP_K):
                row(g * SUBLANES + j, k).start(priority=k % 2)
        return carry

    def wait_rows(g, carry):
        for j in range(SUBLANES):
            for k in range(TOP_K):
                row(g * SUBLANES + j, k).wait()
        return carry

    lax.fori_loop(0, td // SUBLANES, start_rows, 0)
    sh_ref[...] = _swiglu_packed(_rows_from_tiles(hp_ref[...]), wg_ref[...], wu_ref[...], wd_ref[...])
    lax.fori_loop(0, td // SUBLANES, wait_rows, 0)


def _dispatch(hp, dest_tiles, pends, has, n_rows, td, wg, wu, wd):
    t = hp.shape[0]
    tile = hp.shape[1:]
    d, ff = wg.shape
    const = lambda i, pe, ha: (0, 0)
    grid_spec = pltpu.PrefetchScalarGridSpec(
        num_scalar_prefetch=2,
        grid=(t // td,),
        in_specs=[
            pl.BlockSpec((TOP_K * td,), lambda i, pe, ha: (i,), memory_space=pltpu.SMEM),
            pl.BlockSpec((td,) + tile, lambda i, pe, ha: (i, 0, 0)),
            pl.BlockSpec((d, ff), const),
            pl.BlockSpec((d, ff), const),
            pl.BlockSpec((ff, d), const),
        ],
        out_specs=[pl.BlockSpec(memory_space=pl.ANY),
                   pl.BlockSpec((td, d), lambda i, pe, ha: (i, 0))],
        scratch_shapes=[pltpu.VMEM((MOE_BM,) + tile, jnp.uint32), pltpu.SemaphoreType.DMA(()),
                        pltpu.SemaphoreType.DMA(())],
    )
    return pl.pallas_call(
        _dispatch_kernel,
        grid_spec=grid_spec,
        out_shape=[jax.ShapeDtypeStruct((n_rows,) + tile, jnp.uint32),
                   jax.ShapeDtypeStruct((t, d), F32)],
        compiler_params=_cp(("arbitrary",)),
        name="moe_dispatch",
    )(pends, has, dest_tiles, hp, wg, wu, wd)


def _experts_kernel(blk_e_ref, blk_v_ref, blk_src_ref, xs_ref, wg_ref, wu_ref, wd_ref, o_ref,
                    wgb, wub, wdb):
    del blk_src_ref
    b = pl.program_id(0)
    valid = blk_v_ref[b] == 1
    changed = (b == 0) | (blk_e_ref[b] != blk_e_ref[jnp.maximum(b - 1, 0)])

    @pl.when(valid & changed)
    def _cast_weights():
        wgb[...] = wg_ref[...].astype(BF16)
        wub[...] = wu_ref[...].astype(BF16)
        wdb[...] = wd_ref[...].astype(BF16)

    @pl.when(valid)
    def _compute():
        y = _swiglu_packed(_rows_from_tiles(xs_ref[...]), wgb[...], wub[...], wdb[...])
        o_ref[...] = _tiles_from_rows(_pack_rows(y))


def _experts(xs, blk_e, blk_v, blk_src, wg, wu, wd, layer):
    _, ne, d, ff = wg.shape
    bm = MOE_BM
    nblk = blk_e.shape[0]
    tile = xs.shape[1:]
    wmap = lambda b, be, bv, bs: (layer, be[b], 0, 0)
    rows = lambda b, be, bv, bs: (bs[b], 0, 0)
    grid_spec = pltpu.PrefetchScalarGridSpec(
        num_scalar_prefetch=3,
        grid=(nblk,),
        in_specs=[
            pl.BlockSpec((bm,) + tile, rows),
            pl.BlockSpec((None, None, d, ff), wmap),
            pl.BlockSpec((None, None, d, ff), wmap),
            pl.BlockSpec((None, None, ff, d), wmap),
        ],
        out_specs=pl.BlockSpec((bm,) + tile, rows),
        scratch_shapes=[pltpu.VMEM((d, ff), BF16), pltpu.VMEM((d, ff), BF16), pltpu.VMEM((ff, d), BF16)],
    )
    return pl.pallas_call(
        _experts_kernel,
        grid_spec=grid_spec,
        out_shape=jax.ShapeDtypeStruct((nblk * bm,) + tile, jnp.uint32),
        compiler_params=_cp(("arbitrary",)),
        name="moe_experts",
    )(blk_e, blk_v, blk_src, xs, wg, wu, wd)


def _combine_kernel(dcur_ref, dnxt_ref, y_hbm, wts_ref, sh_ref, x_ref, mod_ref, nmod_ref, lng_ref,
                    lnb_ref, *rest, last):
    if last:
        xo_ref, ybuf, sem = rest
    else:
        xo_ref, ho_ref, ybuf, sem = rest
    tt = x_ref.shape[0]
    i = pl.program_id(0)
    slot = i % 2

    def row(dref, slot_, tok, k):
        return pltpu.make_async_copy(y_hbm.at[pl.ds(dref[k * tt + tok], 1)],
                                     ybuf.at[slot_, pl.ds(k * tt + tok, 1)], sem.at[slot_])

    grp = 2 * SUBLANES
    ngrp = tt // grp

    def tile_loop(dref, slot_, op):
        def body(g, carry):
            for j in range(grp):
                for k in range(TOP_K):
                    copy = row(dref, slot_, g * grp + j, k)
                    if op == "start":
                        copy.start(priority=k % 2)
                    else:
                        copy.wait()
            return carry

        lax.fori_loop(0, ngrp, body, 0)

    @pl.when(i == 0)
    def _prime():
        tile_loop(dcur_ref, 0, "start")

    @pl.when(i + 1 < pl.num_programs(0))
    def _prefetch():
        tile_loop(dnxt_ref, 1 - slot, "start")

    tile_loop(dcur_ref, slot, "wait")

    wb = [jnp.broadcast_to(wts_ref[:, k:k + 1], (tt, LANES)) for k in range(TOP_K)]
    lo_cols = [None] * SUBLANES
    hi_cols = [None] * SUBLANES
    for k in range(TOP_K):
        yt = jnp.transpose(ybuf[slot, pl.ds(k * tt, tt)], (1, 0, 2))
        for s in range(SUBLANES):
            lo, hi = _unpack_f32(yt[s])
            lo_cols[s] = lo * wb[k] if k == 0 else lo_cols[s] + lo * wb[k]
            hi_cols[s] = hi * wb[k] if k == 0 else hi_cols[s] + hi * wb[k]
    routed = jnp.concatenate(lo_cols + hi_cols, axis=1)
    xn = _resid_ln(routed + sh_ref[...], x_ref[...], mod_ref, lng_ref, lnb_ref)
    xo_ref[...] = xn
    if not last:
        ho_ref[...] = (xn * (1.0 + nmod_ref[1:2, :]) + nmod_ref[0:1, :]).astype(ho_ref.dtype)


def _combine(ys, dest_tiles, wts, shared, x2, mods, sub, ln_g, ln_b, seq, last):
    t, d = x2.shape
    tt = COMB_TT
    nt = t // tt
    per_b = seq // tt
    nsub = sub if last else sub + 1
    row = lambda i: (i, 0)
    const = lambda i: (0, 0)
    outs = pl.pallas_call(
        functools.partial(_combine_kernel, last=last),
        grid=(nt,),
        in_specs=[
            pl.BlockSpec((tt * TOP_K,), lambda i: (i,), memory_space=pltpu.SMEM),
            pl.BlockSpec((tt * TOP_K,), lambda i: (jnp.minimum(i + 1, nt - 1),), memory_space=pltpu.SMEM),
            pl.BlockSpec(memory_space=pl.ANY),
            pl.BlockSpec((tt, TOP_K), row),
            pl.BlockSpec((tt, d), row),
            pl.BlockSpec((tt, d), row),
            pl.BlockSpec((None, None, 3, d), lambda i: (sub, i // per_b, 0, 0)),
            pl.BlockSpec((None, None, 3, d), lambda i: (nsub, i // per_b, 0, 0)),
            pl.BlockSpec((1, d), const),
            pl.BlockSpec((1, d), const),
        ],
        out_specs=[pl.BlockSpec((tt, d), row)] * (1 if last else 2),
        out_shape=[jax.ShapeDtypeStruct((t, d), F32)] + ([] if last else [jax.ShapeDtypeStruct((t, d), BF16)]),
        scratch_shapes=[pltpu.VMEM((2, TOP_K * tt) + ys.shape[1:], jnp.uint32),
                        pltpu.SemaphoreType.DMA((2,))],
        compiler_params=_cp(("arbitrary",)),
        name="moe_combine",
    )(dest_tiles, dest_tiles, ys, wts, shared, x2, mods, mods, ln_g.reshape(1, d), ln_b.reshape(1, d))
    return (outs[0], None) if last else (outs[0], outs[1])


def _tile_major(a, tile):
    k, t = a.shape
    return a.reshape(k, t // tile, tile).transpose(1, 0, 2).reshape(-1)


def _moe(hp, logits_t, x2, mods, sub, p, layer, seq, last):
    t = hp.shape[0]
    bm = MOE_BM
    eidx, rank, w_t, cnt = _route(logits_t, p["router_bias"])
    counts = cnt[:, 0].astype(jnp.int32)
    padded = (counts + bm - 1) // bm * bm
    pends = jnp.cumsum(padded)
    pstart = pends - padded
    nblk = (t * TOP_K) // bm + N_EXPERTS
    blk_ids = jnp.arange(nblk, dtype=jnp.int32)
    blk_v = (blk_ids * bm < pends[-1]).astype(jnp.int32)
    blk_src = jnp.minimum(blk_ids, pends[-1] // bm - 1).astype(jnp.int32)
    blk_e = jnp.sum((blk_src * bm)[:, None] >= pends[None, :], axis=1).astype(jnp.int32)
    has = (counts > 0).astype(jnp.int32)

    dest = _dest_rows(pstart.astype(jnp.int32), eidx, rank)
    xs, sh = _dispatch(hp, _tile_major(dest, DISP_TD), pends.astype(jnp.int32), has, nblk * bm, DISP_TD,
                       p["ws_gate"].astype(BF16), p["ws_up"].astype(BF16), p["ws_down"].astype(BF16))
    ys = _experts(xs, blk_e, blk_v, blk_src, p["w_gate"], p["w_up"], p["w_down"], layer)
    return _combine(ys, _tile_major(dest, COMB_TT), jnp.transpose(w_t), sh, x2, mods, sub,
                    p["ln_g"], p["ln_b"], seq, last)


def kernel(x, c, ada_w, ada_b, ln_g, ln_b, gm_w_in, gm_b_in, gm_v_ln_g, gm_v_ln_b, gm_w_s, gm_b_s,
           gm_w_out, gm_b_out, at_w_qkv, at_b_qkv, at_sinks, at_w_o, at_b_o, moe_w_router,
           moe_router_bias, moe_w_gate, moe_w_up, moe_w_down, moe_ws_gate, moe_ws_up, moe_ws_down):
    bsz, seq, d = x.shape
    t = bsz * seq
    depth = ada_w.shape[0]
    x2 = x.reshape(t, d)

    rows = 8
    c_pad = jnp.zeros((rows, d), F32).at[:bsz].set(c)
    mods = _adamod(c_pad, ada_w.reshape(depth * 2, d, 3 * d), ada_b.reshape(depth * 2, 3 * d))
    mods = mods.reshape(depth * 2, rows, 3, d)

    h = _modulate(x2, mods, 0, seq)
    for i in range(depth):
        j = i // 2
        wrt = moe_w_router[i]
        if i % 2 == 0:
            z = _mm(h, gm_w_in, j, gm_b_in[j], act="gelu", tm=512, tn=1024, out_dtype=BF16)
            gate_in = (z, gm_v_ln_g[j], gm_v_ln_b[j], gm_w_s[j], gm_b_s[j])
            x2, hp, logits_t = _proj_resid_ln(gate_in, gm_w_out[j].astype(BF16), gm_b_out[j], x2, mods, 2 * i,
                                              ln_g[i, 0], ln_b[i, 0], wrt, seq)
        else:
            qkv = _mm(h, at_w_qkv, j, at_b_qkv[j], act=None, tm=512, tn=1280, out_dtype=BF16)
            o = _attention(qkv, at_sinks[j], seq)
            x2, hp, logits_t = _proj_resid_ln(o, at_w_o[j].astype(BF16), at_b_o[j], x2, mods, 2 * i,
                                              ln_g[i, 0], ln_b[i, 0], wrt, seq)
        p = dict(router_bias=moe_router_bias[i], w_gate=moe_w_gate, w_up=moe_w_up, w_down=moe_w_down,
                 ws_gate=moe_ws_gate[i], ws_up=moe_ws_up[i], ws_down=moe_ws_down[i],
                 ln_g=ln_g[i, 1], ln_b=ln_b[i, 1])
        x2, h = _moe(hp, logits_t, x2, mods, 2 * i + 1, p, i, seq, last=(i == depth - 1))
    return x2.reshape(bsz, seq, d)
```

```python
import functools
import math

import numpy as np
import jax
import jax.numpy as jnp
from jax import lax
from jax.experimental import pallas as pl
from jax.experimental.pallas import tpu as pltpu

F32 = jnp.float32
BF16 = jnp.bfloat16

DEPTH = 4
CHUNK = 128
GMLP_GROUPS = 8
HEAD_DIM = 64
Q_PER_KV = 8
ATT_BLOCK = 128
N_EXPERTS = 64
TOP_K = 8
N_EXPERT_GROUPS = 8
TOPK_GROUPS = 4
ROUTED_SCALE = 2.5
DEEPNORM_ALPHA = (2.0 * DEPTH) ** 0.25
LN_EPS = 1e-5
NEG_INF = -1e30

LANES = 128
VMEM_LIMIT = 56 * 1024 * 1024

ROW_TM = 512
GELU_TN = 1024
QKV_TN = 1280
PROJ_TK = 1024
EPILOGUE_SLAB = 128
ADAMOD_TN = 768
ROUTE_TT = 1024
MOE_BM = 512
COMB_TT = 128
DISP_TD = 256


def _cp(sem, vmem=VMEM_LIMIT):
    return pltpu.CompilerParams(dimension_semantics=sem, vmem_limit_bytes=vmem)


def _layer_norm(x, g, b):
    mu = jnp.mean(x, axis=-1, keepdims=True)
    xc = x - mu
    var = jnp.mean(xc * xc, axis=-1, keepdims=True)
    return xc * lax.rsqrt(var + LN_EPS) * g + b


def _alibi_slopes(n):
    p = 2 ** int(math.floor(math.log2(n)))
    base = [2.0 ** (-8.0 * (i + 1) / p) for i in range(p)]
    extra = [2.0 ** (-4.0 * (2 * i + 1) / p) for i in range(n - p)]
    return [float(np.float32(s)) for s in base + extra]


def _adamod_kernel(c_ref, w_ref, b_ref, o_ref):
    c = c_ref[...]
    ca = (c * jax.nn.sigmoid(c)).astype(BF16)
    o_ref[...] = jnp.dot(ca, w_ref[...].astype(BF16), preferred_element_type=F32) + b_ref[...]


def _adamod(c_pad, ada_w, ada_b):
    ns, d, d3 = ada_w.shape
    rows = c_pad.shape[0]
    tn = ADAMOD_TN
    return pl.pallas_call(
        _adamod_kernel,
        grid=(ns, d3 // tn),
        in_specs=[
            pl.BlockSpec((rows, d), lambda s, j: (0, 0)),
            pl.BlockSpec((None, d, tn), lambda s, j: (s, 0, j)),
            pl.BlockSpec((None, 1, tn), lambda s, j: (s, 0, j)),
        ],
        out_specs=pl.BlockSpec((None, rows, tn), lambda s, j: (s, 0, j)),
        out_shape=jax.ShapeDtypeStruct((ns, rows, d3), F32),
        compiler_params=_cp(("arbitrary", "arbitrary")),
        name="adamod",
    )(c_pad, ada_w, ada_b.reshape(ns, 1, d3))


def _modulate_kernel(x_ref, mod_ref, h_ref):
    h = x_ref[...] * (1.0 + mod_ref[1:2, :]) + mod_ref[0:1, :]
    h_ref[...] = h.astype(h_ref.dtype)


def _modulate(x2, mods, sub, seq):
    t, d = x2.shape
    tm = ROW_TM
    per_b = seq // tm
    return pl.pallas_call(
        _modulate_kernel,
        grid=(t // tm,),
        in_specs=[
            pl.BlockSpec((tm, d), lambda i: (i, 0)),
            pl.BlockSpec((None, None, 3, d), lambda i: (sub, i // per_b, 0, 0)),
        ],
        out_specs=pl.BlockSpec((tm, d), lambda i: (i, 0)),
        out_shape=jax.ShapeDtypeStruct((t, d), BF16),
        compiler_params=_cp(("arbitrary",)),
        name="modulate",
    )(x2, mods)


def _mm_kernel(a_ref, w_ref, b_ref, o_ref, wb_ref, *, act):
    @pl.when(pl.program_id(1) == 0)
    def _cast_weights():
        wb_ref[...] = w_ref[...].astype(BF16)

    acc = jnp.dot(a_ref[...], wb_ref[...], preferred_element_type=F32) + b_ref[...]
    if act == "gelu":
        acc = 0.5 * acc * (1.0 + lax.erf(acc * np.float32(math.sqrt(0.5))))
    o_ref[...] = acc.astype(o_ref.dtype)


def _mm(a, w_stack, layer, b, *, act, tm, tn, out_dtype):
    m, k = a.shape
    n = w_stack.shape[2]
    return pl.pallas_call(
        functools.partial(_mm_kernel, act=act),
        grid=(n // tn, m // tm),
        in_specs=[
            pl.BlockSpec((tm, k), lambda j, i: (i, 0)),
            pl.BlockSpec((None, k, tn), lambda j, i: (layer, 0, j)),
            pl.BlockSpec((1, tn), lambda j, i: (0, j)),
        ],
        out_specs=pl.BlockSpec((tm, tn), lambda j, i: (i, j)),
        out_shape=jax.ShapeDtypeStruct((m, n), out_dtype),
        scratch_shapes=[pltpu.VMEM((k, tn), BF16)],
        compiler_params=_cp(("arbitrary", "arbitrary")),
        name="mm_" + (act or "bias"),
    )(a, w_stack, b.reshape(1, n))


def _attn_kernel(sink_ref, q_ref, kc_ref, vc_ref, kp_ref, vp_ref, o_ref, bias_ref, p_ref,
                 *, nb, n_kv, slopes):
    blk = ATT_BLOCK
    pairs = Q_PER_KV // 2
    i = pl.program_id(0)

    @pl.when(i == 0)
    def _build_bias():
        qi = lax.broadcasted_iota(jnp.int32, (blk, 2 * blk), 0)
        kj = lax.broadcasted_iota(jnp.int32, (blk, 2 * blk), 1)
        dist = qi + blk - kj
        valid = (dist >= 0) & (dist < blk)
        distf = dist.astype(F32)
        for g in range(n_kv):
            for p in range(pairs):
                for j in range(2):
                    slope = slopes[g * Q_PER_KV + 2 * p + j]
                    base = jnp.where(valid, -slope * distf, NEG_INF)
                    rs = slice(p * blk, (p + 1) * blk)
                    cs = slice(j * 2 * blk, (j + 1) * 2 * blk)
                    bias_ref[0, g, rs, cs] = base
                    bias_ref[1, g, rs, cs] = jnp.where(kj < blk, NEG_INF, base)

    first = (i % nb == 0).astype(jnp.int32)
    lane = lax.broadcasted_iota(jnp.int32, (2 * blk, LANES), 1)
    lo = lane < HEAD_DIM
    lo_q = lax.broadcasted_iota(jnp.int32, (blk, LANES), 1) < HEAD_DIM

    def blockdiag(prev_ref, cur_ref, g, scale):
        c, half = g // 2, g % 2
        cs = slice(c * LANES, (c + 1) * LANES)
        t2 = jnp.concatenate([prev_ref[:, cs], cur_ref[:, cs]], axis=0).astype(F32) * scale
        rolled = pltpu.roll(t2, HEAD_DIM, 1)
        if half == 0:
            a = jnp.where(lo, t2, 0.0)
            b = jnp.where(lo, 0.0, rolled)
        else:
            a = jnp.where(lo, rolled, 0.0)
            b = jnp.where(lo, 0.0, t2)
        return jnp.concatenate([a, b], axis=0).astype(BF16)

    for g in range(n_kv):
        kbd = blockdiag(kp_ref, kc_ref, g, HEAD_DIM ** -0.5)
        vbd = blockdiag(vp_ref, vc_ref, g, 1.0)
        qg = jnp.concatenate(
            [q_ref[:, (g * pairs + p) * LANES:(g * pairs + p + 1) * LANES] for p in range(pairs)], axis=0)
        s = lax.dot_general(qg, kbd, (((1,), (1,)), ((), ())), preferred_element_type=F32)
        s = s + bias_ref[first, g]
        inv = []
        for p in range(pairs):
            for j in range(2):
                rs = slice(p * blk, (p + 1) * blk)
                cs = slice(j * 2 * blk, (j + 1) * 2 * blk)
                sub = s[rs, cs]
                sink = sink_ref[g * Q_PER_KV + 2 * p + j]
                m = jnp.maximum(jnp.max(sub, axis=-1, keepdims=True), sink)
                e = jnp.exp(sub - m)
                denom = jnp.sum(e, axis=-1, keepdims=True) + jnp.exp(sink - m)
                p_ref[rs, cs] = e.astype(BF16)
                inv.append(1.0 / denom)
        o = jnp.dot(p_ref[...], vbd, preferred_element_type=F32)
        for p in range(pairs):
            scale = jnp.where(lo_q, inv[2 * p], inv[2 * p + 1])
            col = (g * pairs + p) * LANES
            o_ref[:, col:col + LANES] = (o[p * blk:(p + 1) * blk] * scale).astype(o_ref.dtype)


def _attention(qkv, sinks, seq):
    t, w = qkv.shape
    blk = ATT_BLOCK
    nb = seq // blk
    n_kv = (w // HEAD_DIM) // (Q_PER_KV + 2)
    dq = n_kv * Q_PER_KV * HEAD_DIM
    dkv = n_kv * HEAD_DIM
    kcol = dq // dkv
    pairs = Q_PER_KV // 2
    slopes = _alibi_slopes(n_kv * Q_PER_KV)

    def prev(i):
        return jnp.where(i % nb == 0, i, i - 1)

    return pl.pallas_call(
        functools.partial(_attn_kernel, nb=nb, n_kv=n_kv, slopes=slopes),
        grid=(t // blk,),
        in_specs=[
            pl.BlockSpec(memory_space=pltpu.SMEM),
            pl.BlockSpec((blk, dq), lambda i: (i, 0)),
            pl.BlockSpec((blk, dkv), lambda i: (i, kcol)),
            pl.BlockSpec((blk, dkv), lambda i: (i, kcol + 1)),
            pl.BlockSpec((blk, dkv), lambda i: (prev(i), kcol)),
            pl.BlockSpec((blk, dkv), lambda i: (prev(i), kcol + 1)),
        ],
        out_specs=pl.BlockSpec((blk, dq), lambda i: (i, 0)),
        out_shape=jax.ShapeDtypeStruct((t, dq), BF16),
        scratch_shapes=[
            pltpu.VMEM((2, n_kv, pairs * blk, 4 * blk), F32),
            pltpu.VMEM((pairs * blk, 4 * blk), BF16),
        ],
        compiler_params=_cp(("arbitrary",)),
        name="swa_attention",
    )(sinks, qkv, qkv, qkv, qkv, qkv)


SUBLANES = 8


def _rows_from_tiles(x3):
    xt = jnp.transpose(x3, (1, 0, 2))
    return jnp.concatenate([xt[s] for s in range(x3.shape[1])], axis=1)


def _tiles_from_rows(x2):
    xt = jnp.stack([x2[:, s * LANES:(s + 1) * LANES] for s in range(x2.shape[1] // LANES)], axis=0)
    return jnp.transpose(xt, (1, 0, 2))


def _pack_rows(h):
    half = h.shape[1] // 2
    bits = lax.bitcast_convert_type(h.astype(BF16).astype(F32), jnp.uint32)
    return (bits[:, :half] >> 16) | (bits[:, half:] & jnp.uint32(0xFFFF0000))


def _unpack_f32(p):
    lo = lax.bitcast_convert_type(p << 16, F32)
    hi = lax.bitcast_convert_type(p & jnp.uint32(0xFFFF0000), F32)
    return lo, hi


def _unpack_rows(p):
    lo, hi = _unpack_f32(p)
    return lo.astype(BF16), hi.astype(BF16)


def _swiglu_packed(p, wg, wu, wd):
    lo, hi = _unpack_rows(p)
    half = p.shape[1]
    gt = (jnp.dot(lo, wg[:half], preferred_element_type=F32)
          + jnp.dot(hi, wg[half:], preferred_element_type=F32))
    up = (jnp.dot(lo, wu[:half], preferred_element_type=F32)
          + jnp.dot(hi, wu[half:], preferred_element_type=F32))
    act = (gt * jax.nn.sigmoid(gt) * up).astype(BF16)
    return jnp.dot(act, wd, preferred_element_type=F32)


def _resid_ln(y, x, mod_ref, lng_ref, lnb_ref):
    r = DEEPNORM_ALPHA * x + (1.0 + mod_ref[2:3, :]) * y
    return _layer_norm(r, lng_ref[...], lnb_ref[...])


def _proj_kernel(a_ref, w_ref, *rest, nk, slab):
    k = pl.program_id(1)
    acc_ref = rest[-1]

    @pl.when(k == 0)
    def _zero():
        acc_ref[...] = jnp.zeros_like(acc_ref)

    acc_ref[...] += jnp.dot(a_ref[...], w_ref[...], preferred_element_type=F32)
    _proj_finish(k == nk - 1, *rest, slab=slab)


def _gate_proj_kernel(u_ref, v_ref, vall_ref, vg_ref, vb_ref, ws_ref, bs_ref, w_ref, *rest, nk, slab):
    k = pl.program_id(1)
    acc_ref, mu_ref, rstd_ref = rest[-3:]
    tm, gw = u_ref.shape

    @pl.when(k == 0)
    def _start():
        acc_ref[...] = jnp.zeros_like(acc_ref)
        for r0 in range(0, tm, slab):
            rs = slice(r0, r0 + slab)
            v = vall_ref[rs, :].astype(F32)
            mu = jnp.mean(v, axis=-1, keepdims=True)
            var = jnp.mean(v * v, axis=-1, keepdims=True) - mu * mu
            mu_ref[rs, :] = jnp.broadcast_to(mu, (slab, LANES))
            rstd_ref[rs, :] = jnp.broadcast_to(lax.rsqrt(var + LN_EPS), (slab, LANES))

    row = lax.broadcasted_iota(jnp.int32, (CHUNK, CHUNK), 0)
    col = lax.broadcasted_iota(jnp.int32, (CHUNK, CHUNK), 1)
    wsg = jnp.where(row >= col, ws_ref[k], 0.0).astype(BF16)
    bs = bs_ref[k][:, 0:1]
    vn = ((v_ref[...].astype(F32) - mu_ref[:, 0:1]) * rstd_ref[:, 0:1] * vg_ref[...]
          + vb_ref[...]).astype(BF16)
    parts = []
    for c in range(tm // CHUNK):
        rs = slice(c * CHUNK, (c + 1) * CHUNK)
        s = jnp.dot(wsg, vn[rs], preferred_element_type=F32) + bs
        parts.append((u_ref[rs, :].astype(F32) * s).astype(BF16))
    a = jnp.concatenate(parts, axis=0)
    acc_ref[...] += jnp.dot(a, w_ref[...], preferred_element_type=F32)
    _proj_finish(k == nk - 1, *rest[:-2], slab=slab)


def _proj_finish(is_last, b_ref, x_ref, mod_ref, nmod_ref, lng_ref, lnb_ref, wrh_ref, wrl_ref,
                 xo_ref, hp_ref, lt_ref, acc_ref, *, slab):
    @pl.when(is_last)
    def _finish():
        for r0 in range(0, acc_ref.shape[0], slab):
            rs = slice(r0, r0 + slab)
            xn = _resid_ln(acc_ref[rs, :] + b_ref[...], x_ref[rs, :], mod_ref, lng_ref, lnb_ref)
            xo_ref[rs, :] = xn
            h = xn * (1.0 + nmod_ref[1:2, :]) + nmod_ref[0:1, :]
            hp_ref[rs] = _tiles_from_rows(_pack_rows(h))
            h_hi = h.astype(BF16)
            h_lo = (h - h_hi.astype(F32)).astype(BF16)
            lg = (jnp.dot(h_hi, wrh_ref[...], preferred_element_type=F32)
                  + (jnp.dot(h_hi, wrl_ref[...], preferred_element_type=F32)
                     + jnp.dot(h_lo, wrh_ref[...], preferred_element_type=F32)))
            lt_ref[:, rs] = jnp.transpose(lg)[:lt_ref.shape[0]]


def _proj_resid_ln(lhs, w, b, x2, mods, sub, ln_g, ln_b, w_router, seq):
    t, d = x2.shape
    tm = ROW_TM
    per_b = seq // tm
    ne = w_router.shape[1]
    wr = jnp.zeros((d, LANES), F32).at[:, :ne].set(w_router)
    wr_hi = wr.astype(BF16)
    wr_lo = (wr - wr_hi.astype(F32)).astype(BF16)
    row = lambda i, k: (i, 0)
    const = lambda i, k: (0, 0)
    const3 = lambda i, k: (0, 0, 0)
    if isinstance(lhs, tuple):
        z, vg, vb, w_s, b_s = lhs
        half = z.shape[1] // 2
        nk = GMLP_GROUPS
        tk = half // nk
        body = _gate_proj_kernel
        lhs_args = (z, z, z, vg.reshape(1, half), vb.reshape(1, half), w_s,
                    jnp.broadcast_to(b_s[:, :, None], b_s.shape + (LANES,)))
        lhs_specs = [
            pl.BlockSpec((tm, tk), lambda i, k: (i, k)),
            pl.BlockSpec((tm, tk), lambda i, k: (i, nk + k)),
            pl.BlockSpec((tm, half), lambda i, k: (i, 1)),
            pl.BlockSpec((1, tk), lambda i, k: (0, k)),
            pl.BlockSpec((1, tk), lambda i, k: (0, k)),
            pl.BlockSpec(w_s.shape, const3),
            pl.BlockSpec(b_s.shape + (LANES,), const3),
        ]
        extra_scratch = [pltpu.VMEM((tm, LANES), F32), pltpu.VMEM((tm, LANES), F32)]
    else:
        kdim = lhs.shape[1]
        tk = min(kdim, PROJ_TK)
        nk = kdim // tk
        body = _proj_kernel
        lhs_args = (lhs,)
        lhs_specs = [pl.BlockSpec((tm, tk), lambda i, k: (i, k))]
        extra_scratch = []
    return pl.pallas_call(
        functools.partial(body, nk=nk, slab=EPILOGUE_SLAB),
        grid=(t // tm, nk),
        in_specs=lhs_specs + [
            pl.BlockSpec((tk, d), lambda i, k: (k, 0)),
            pl.BlockSpec((1, d), const),
            pl.BlockSpec((tm, d), row),
            pl.BlockSpec((None, None, 3, d), lambda i, k: (sub, i // per_b, 0, 0)),
            pl.BlockSpec((None, None, 3, d), lambda i, k: (sub + 1, i // per_b, 0, 0)),
            pl.BlockSpec((1, d), const),
            pl.BlockSpec((1, d), const),
            pl.BlockSpec((d, LANES), const),
            pl.BlockSpec((d, LANES), const),
        ],
        out_specs=[
            pl.BlockSpec((tm, d), row),
            pl.BlockSpec((tm, SUBLANES, LANES), lambda i, k: (i, 0, 0)),
            pl.BlockSpec((ne, tm), lambda i, k: (0, i)),
        ],
        out_shape=[
            jax.ShapeDtypeStruct((t, d), F32),
            jax.ShapeDtypeStruct((t, SUBLANES, LANES), jnp.uint32),
            jax.ShapeDtypeStruct((ne, t), F32),
        ],
        scratch_shapes=[pltpu.VMEM((tm, d), F32)] + extra_scratch,
        compiler_params=_cp(("arbitrary", "arbitrary")),
        name="proj_resid_ln",
    )(*lhs_args, w, b.reshape(1, d), x2, mods, mods, ln_g.reshape(1, d), ln_b.reshape(1, d), wr_hi, wr_lo)


def _route_kernel(lg_ref, bias_ref, eidx_ref, rank_ref, w_ref, cnt_ref, tri_ref, carry_ref):
    i = pl.program_id(0)
    ne, tt = lg_ref.shape
    gsz = ne // N_EXPERT_GROUPS
    neg = -jnp.inf

    @pl.when(i == 0)
    def _init():
        r = lax.broadcasted_iota(jnp.int32, (tt, tt), 0)
        c = lax.broadcasted_iota(jnp.int32, (tt, tt), 1)
        tri_ref[...] = jnp.where(r < c, 1.0, 0.0).astype(BF16)
        carry_ref[...] = jnp.zeros_like(carry_ref)

    def amax(v, ids, limit):
        m = jnp.max(v, axis=0, keepdims=True)
        idx = jnp.min(jnp.where(v == m, ids, limit), axis=0, keepdims=True)
        return m, idx

    s = jax.nn.sigmoid(lg_ref[...])
    sel = s + bias_ref[:, 0:1]

    ids_g = lax.broadcasted_iota(jnp.int32, (gsz, tt), 0).astype(F32)
    gscore = []
    for g in range(N_EXPERT_GROUPS):
        xg = sel[g * gsz:(g + 1) * gsz]
        m1, i1 = amax(xg, ids_g, float(gsz))
        m2 = jnp.max(jnp.where(ids_g == i1, neg, xg), axis=0, keepdims=True)
        gscore.append(m1 + m2)
    work = jnp.concatenate(gscore, axis=0)
    ids_n = lax.broadcasted_iota(jnp.int32, (N_EXPERT_GROUPS, tt), 0).astype(F32)
    gsel = jnp.zeros((N_EXPERT_GROUPS, tt), F32)
    for _ in range(TOPK_GROUPS):
        _, ig = amax(work, ids_n, float(N_EXPERT_GROUPS))
        pick = ids_n == ig
        gsel = jnp.where(pick, 1.0, gsel)
        work = jnp.where(pick, neg, work)
    work = jnp.concatenate(
        [jnp.where(gsel[g:g + 1] > 0.5, sel[g * gsz:(g + 1) * gsz], NEG_INF)
         for g in range(N_EXPERT_GROUPS)], axis=0)

    ids_e = lax.broadcasted_iota(jnp.int32, (ne, tt), 0).astype(F32)
    chosen = jnp.zeros((ne, tt), F32)
    picked, svals = [], []
    for _ in range(TOP_K):
        _, ik = amax(work, ids_e, float(ne))
        pick = ids_e == ik
        chosen = jnp.where(pick, 1.0, chosen)
        work = jnp.where(pick, neg, work)
        picked.append(ik)
        svals.append(jnp.sum(jnp.where(pick, s, 0.0), axis=0, keepdims=True))
    denom = svals[0]
    for k in range(1, TOP_K):
        denom = denom + svals[k]

    prefix = jnp.dot(chosen.astype(BF16), tri_ref[...], preferred_element_type=F32)
    rank_dense = prefix + carry_ref[:, 0:1]
    carry_ref[...] = carry_ref[...] + jnp.sum(chosen, axis=1, keepdims=True)
    cnt_ref[...] = carry_ref[...]
    for k in range(TOP_K):
        pick = ids_e == picked[k]
        eidx_ref[k:k + 1, :] = picked[k].astype(jnp.int32)
        rank_ref[k:k + 1, :] = jnp.sum(jnp.where(pick, rank_dense, 0.0), axis=0,
                                       keepdims=True).astype(jnp.int32)
        w_ref[k:k + 1, :] = svals[k] / denom * ROUTED_SCALE


def _route(logits_t, router_bias):
    ne, t = logits_t.shape
    tt = ROUTE_TT
    bias = jnp.broadcast_to(router_bias.astype(F32)[:, None], (ne, LANES))
    tok = lambda i: (0, i)
    const = lambda i: (0, 0)
    return pl.pallas_call(
        _route_kernel,
        grid=(t // tt,),
        in_specs=[pl.BlockSpec((ne, tt), tok), pl.BlockSpec((ne, LANES), const)],
        out_specs=[
            pl.BlockSpec((TOP_K, tt), tok),
            pl.BlockSpec((TOP_K, tt), tok),
            pl.BlockSpec((TOP_K, tt), tok),
            pl.BlockSpec((ne, LANES), const),
        ],
        out_shape=[
            jax.ShapeDtypeStruct((TOP_K, t), jnp.int32),
            jax.ShapeDtypeStruct((TOP_K, t), jnp.int32),
            jax.ShapeDtypeStruct((TOP_K, t), F32),
            jax.ShapeDtypeStruct((ne, LANES), F32),
        ],
        scratch_shapes=[pltpu.VMEM((tt, tt), BF16), pltpu.VMEM((ne, LANES), F32)],
        compiler_params=_cp(("arbitrary",)),
        name="moe_route",
    )(logits_t, bias)


def _dest_kernel(pstart_ref, eidx_ref, rank_ref, dest_ref):
    e = eidx_ref[...]
    base = jnp.zeros_like(e)
    for x in range(N_EXPERTS):
        base = jnp.where(e == x, pstart_ref[x], base)
    dest_ref[...] = rank_ref[...] + base


def _dest_rows(pstart, eidx, rank):
    return pl.pallas_call(
        _dest_kernel,
        in_specs=[pl.BlockSpec(memory_space=pltpu.SMEM),
                  pl.BlockSpec(memory_space=pltpu.VMEM),
                  pl.BlockSpec(memory_space=pltpu.VMEM)],
        out_specs=pl.BlockSpec(memory_space=pltpu.VMEM),
        out_shape=jax.ShapeDtypeStruct(eidx.shape, jnp.int32),
        name="moe_dest",
    )(pstart, eidx, rank)


def _dispatch_kernel(pend_ref, has_ref, dest_ref, hp_ref, wg_ref, wu_ref, wd_ref, xs_hbm, sh_ref,
                     zbuf, sem, zsem):
    i = pl.program_id(0)
    td = hp_ref.shape[0]
    bm = zbuf.shape[0]

    @pl.when(i == 0)
    def _zero_tails():
        zbuf[...] = jnp.zeros_like(zbuf)

        def tail(e):
            start_row = pl.multiple_of(pend_ref[e] - bm, bm)
            return pltpu.make_async_copy(zbuf, xs_hbm.at[pl.ds(start_row, bm)], zsem)

        def start(e, carry):
            @pl.when(has_ref[e] == 1)
            def _():
                tail(e).start()
            return carry

        def wait(e, carry):
            @pl.when(has_ref[e] == 1)
            def _():
                tail(e).wait()
            return carry

        lax.fori_loop(0, N_EXPERTS, start, 0)
        lax.fori_loop(0, N_EXPERTS, wait, 0)

    def row(tok, k):
        return pltpu.make_async_copy(hp_ref.at[pl.ds(tok, 1)],
                                     xs_hbm.at[pl.ds(dest_ref[k * td + tok], 1)], sem)

    def start_rows(g, carry):
        for j in range(SUBLANES):
            for k in range(TOP_K):
                row(g * SUBLANES + j, k).start(priority=k % 2)
        return carry

    def wait_rows(g, carry):
        for j in range(SUBLANES):
            for k in range(TOP_K):
                row(g * SUBLANES + j, k).wait()
        return carry

    lax.fori_loop(0, td // SUBLANES, start_rows, 0)
    sh_ref[...] = _swiglu_packed(_rows_from_tiles(hp_ref[...]), wg_ref[...], wu_ref[...], wd_ref[...])
    lax.fori_loop(0, td // SUBLANES, wait_rows, 0)


def _dispatch(hp, dest_tiles, pends, has, n_rows, td, wg, wu, wd):
    t = hp.shape[0]
    tile = hp.shape[1:]
    d, ff = wg.shape
    const = lambda i, pe, ha: (0, 0)
    grid_spec = pltpu.PrefetchScalarGridSpec(
        num_scalar_prefetch=2,
        grid=(t // td,),
        in_specs=[
            pl.BlockSpec((TOP_K * td,), lambda i, pe, ha: (i,), memory_space=pltpu.SMEM),
            pl.BlockSpec((td,) + tile, lambda i, pe, ha: (i, 0, 0)),
            pl.BlockSpec((d, ff), const),
            pl.BlockSpec((d, ff), const),
            pl.BlockSpec((ff, d), const),
        ],
        out_specs=[pl.BlockSpec(memory_space=pl.ANY),
                   pl.BlockSpec((td, d), lambda i, pe, ha: (i, 0))],
        scratch_shapes=[pltpu.VMEM((MOE_BM,) + tile, jnp.uint32), pltpu.SemaphoreType.DMA(()),
                        pltpu.SemaphoreType.DMA(())],
    )
    return pl.pallas_call(
        _dispatch_kernel,
        grid_spec=grid_spec,
        out_shape=[jax.ShapeDtypeStruct((n_rows,) + tile, jnp.uint32),
                   jax.ShapeDtypeStruct((t, d), F32)],
        compiler_params=_cp(("arbitrary",)),
        name="moe_dispatch",
    )(pends, has, dest_tiles, hp, wg, wu, wd)


def _experts_kernel(blk_e_ref, blk_v_ref, blk_src_ref, xs_ref, wg_ref, wu_ref, wd_ref, o_ref,
                    wgb, wub, wdb):
    del blk_src_ref
    b = pl.program_id(0)
    valid = blk_v_ref[b] == 1
    changed = (b == 0) | (blk_e_ref[b] != blk_e_ref[jnp.maximum(b - 1, 0)])

    @pl.when(valid & changed)
    def _cast_weights():
        wgb[...] = wg_ref[...].astype(BF16)
        wub[...] = wu_ref[...].astype(BF16)
        wdb[...] = wd_ref[...].astype(BF16)

    @pl.when(valid)
    def _compute():
        y = _swiglu_packed(_rows_from_tiles(xs_ref[...]), wgb[...], wub[...], wdb[...])
        o_ref[...] = _tiles_from_rows(_pack_rows(y))


def _experts(xs, blk_e, blk_v, blk_src, wg, wu, wd, layer):
    _, ne, d, ff = wg.shape
    bm = MOE_BM
    nblk = blk_e.shape[0]
    tile = xs.shape[1:]
    wmap = lambda b, be, bv, bs: (layer, be[b], 0, 0)
    rows = lambda b, be, bv, bs: (bs[b], 0, 0)
    grid_spec = pltpu.PrefetchScalarGridSpec(
        num_scalar_prefetch=3,
        grid=(nblk,),
        in_specs=[
            pl.BlockSpec((bm,) + tile, rows),
            pl.BlockSpec((None, None, d, ff), wmap),
            pl.BlockSpec((None, None, d, ff), wmap),
            pl.BlockSpec((None, None, ff, d), wmap),
        ],
        out_specs=pl.BlockSpec((bm,) + tile, rows),
        scratch_shapes=[pltpu.VMEM((d, ff), BF16), pltpu.VMEM((d, ff), BF16), pltpu.VMEM((ff, d), BF16)],
    )
    return pl.pallas_call(
        _experts_kernel,
        grid_spec=grid_spec,
        out_shape=jax.ShapeDtypeStruct((nblk * bm,) + tile, jnp.uint32),
        compiler_params=_cp(("arbitrary",)),
        name="moe_experts",
    )(blk_e, blk_v, blk_src, xs, wg, wu, wd)


def _combine_kernel(dcur_ref, dnxt_ref, y_hbm, wts_ref, sh_ref, x_ref, mod_ref, nmod_ref, lng_ref,
                    lnb_ref, *rest, last):
    if last:
        xo_ref, ybuf, sem = rest
    else:
        xo_ref, ho_ref, ybuf, sem = rest
    tt = x_ref.shape[0]
    i = pl.program_id(0)
    slot = i % 2

    def row(dref, slot_, tok, k):
        return pltpu.make_async_copy(y_hbm.at[pl.ds(dref[k * tt + tok], 1)],
                                     ybuf.at[slot_, pl.ds(k * tt + tok, 1)], sem.at[slot_])

    grp = 2 * SUBLANES
    ngrp = tt // grp

    def tile_loop(dref, slot_, op):
        def body(g, carry):
            for j in range(grp):
                for k in range(TOP_K):
                    copy = row(dref, slot_, g * grp + j, k)
                    if op == "start":
                        copy.start(priority=k % 2)
                    else:
                        copy.wait()
            return carry

        lax.fori_loop(0, ngrp, body, 0)

    @pl.when(i == 0)
    def _prime():
        tile_loop(dcur_ref, 0, "start")

    @pl.when(i + 1 < pl.num_programs(0))
    def _prefetch():
        tile_loop(dnxt_ref, 1 - slot, "start")

    tile_loop(dcur_ref, slot, "wait")

    wb = [jnp.broadcast_to(wts_ref[:, k:k + 1], (tt, LANES)) for k in range(TOP_K)]
    lo_cols = [None] * SUBLANES
    hi_cols = [None] * SUBLANES
    for k in range(TOP_K):
        yt = jnp.transpose(ybuf[slot, pl.ds(k * tt, tt)], (1, 0, 2))
        for s in range(SUBLANES):
            lo, hi = _unpack_f32(yt[s])
            lo_cols[s] = lo * wb[k] if k == 0 else lo_cols[s] + lo * wb[k]
            hi_cols[s] = hi * wb[k] if k == 0 else hi_cols[s] + hi * wb[k]
    routed = jnp.concatenate(lo_cols + hi_cols, axis=1)
    xn = _resid_ln(routed + sh_ref[...], x_ref[...], mod_ref, lng_ref, lnb_ref)
    xo_ref[...] = xn
    if not last:
        ho_ref[...] = (xn * (1.0 + nmod_ref[1:2, :]) + nmod_ref[0:1, :]).astype(ho_ref.dtype)


def _combine(ys, dest_tiles, wts, shared, x2, mods, sub, ln_g, ln_b, seq, last):
    t, d = x2.shape
    tt = COMB_TT
    nt = t // tt
    per_b = seq // tt
    nsub = sub if last else sub + 1
    row = lambda i: (i, 0)
    const = lambda i: (0, 0)
    outs = pl.pallas_call(
        functools.partial(_combine_kernel, last=last),
        grid=(nt,),
        in_specs=[
            pl.BlockSpec((tt * TOP_K,), lambda i: (i,), memory_space=pltpu.SMEM),
            pl.BlockSpec((tt * TOP_K,), lambda i: (jnp.minimum(i + 1, nt - 1),), memory_space=pltpu.SMEM),
            pl.BlockSpec(memory_space=pl.ANY),
            pl.BlockSpec((tt, TOP_K), row),
            pl.BlockSpec((tt, d), row),
            pl.BlockSpec((tt, d), row),
            pl.BlockSpec((None, None, 3, d), lambda i: (sub, i // per_b, 0, 0)),
            pl.BlockSpec((None, None, 3, d), lambda i: (nsub, i // per_b, 0, 0)),
            pl.BlockSpec((1, d), const),
            pl.BlockSpec((1, d), const),
        ],
        out_specs=[pl.BlockSpec((tt, d), row)] * (1 if last else 2),
        out_shape=[jax.ShapeDtypeStruct((t, d), F32)] + ([] if last else [jax.ShapeDtypeStruct((t, d), BF16)]),
        scratch_shapes=[pltpu.VMEM((2, TOP_K * tt) + ys.shape[1:], jnp.uint32),
                        pltpu.SemaphoreType.DMA((2,))],
        compiler_params=_cp(("arbitrary",)),
        name="moe_combine",
    )(dest_tiles, dest_tiles, ys, wts, shared, x2, mods, mods, ln_g.reshape(1, d), ln_b.reshape(1, d))
    return (outs[0], None) if last else (outs[0], outs[1])


def _tile_major(a, tile):
    k, t = a.shape
    return a.reshape(k, t // tile, tile).transpose(1, 0, 2).reshape(-1)


def _moe(hp, logits_t, x2, mods, sub, p, layer, seq, last):
    t = hp.shape[0]
    bm = MOE_BM
    eidx, rank, w_t, cnt = _route(logits_t, p["router_bias"])
    counts = cnt[:, 0].astype(jnp.int32)
    padded = (counts + bm - 1) // bm * bm
    pends = jnp.cumsum(padded)
    pstart = pends - padded
    nblk = (t * TOP_K) // bm + N_EXPERTS
    blk_ids = jnp.arange(nblk, dtype=jnp.int32)
    blk_v = (blk_ids * bm < pends[-1]).astype(jnp.int32)
    blk_src = jnp.minimum(blk_ids, pends[-1] // bm - 1).astype(jnp.int32)
    blk_e = jnp.sum((blk_src * bm)[:, None] >= pends[None, :], axis=1).astype(jnp.int32)
    has = (counts > 0).astype(jnp.int32)

    dest = _dest_rows(pstart.astype(jnp.int32), eidx, rank)
    xs, sh = _dispatch(hp, _tile_major(dest, DISP_TD), pends.astype(jnp.int32), has, nblk * bm, DISP_TD,
                       p["ws_gate"].astype(BF16), p["ws_up"].astype(BF16), p["ws_down"].astype(BF16))
    ys = _experts(xs, blk_e, blk_v, blk_src, p["w_gate"], p["w_up"], p["w_down"], layer)
    return _combine(ys, _tile_major(dest, COMB_TT), jnp.transpose(w_t), sh, x2, mods, sub,
                    p["ln_g"], p["ln_b"], seq, last)


def kernel(x, c, ada_w, ada_b, ln_g, ln_b, gm_w_in, gm_b_in, gm_v_ln_g, gm_v_ln_b, gm_w_s, gm_b_s,
           gm_w_out, gm_b_out, at_w_qkv, at_b_qkv, at_sinks, at_w_o, at_b_o, moe_w_router,
           moe_router_bias, moe_w_gate, moe_w_up, moe_w_down, moe_ws_gate, moe_ws_up, moe_ws_down):
    bsz, seq, d = x.shape
    t = bsz * seq
    depth = ada_w.shape[0]
    x2 = x.reshape(t, d)

    rows = 8
    c_pad = jnp.zeros((rows, d), F32).at[:bsz].set(c)
    mods = _adamod(c_pad, ada_w.reshape(depth * 2, d, 3 * d), ada_b.reshape(depth * 2, 3 * d))
    mods = mods.reshape(depth * 2, rows, 3, d)

    h = _modulate(x2, mods, 0, seq)
    for i in range(depth):
        j = i // 2
        wrt = moe_w_router[i]
        if i % 2 == 0:
            z = _mm(h, gm_w_in, j, gm_b_in[j], act="gelu", tm=ROW_TM, tn=GELU_TN, out_dtype=BF16)
            gate_in = (z, gm_v_ln_g[j], gm_v_ln_b[j], gm_w_s[j], gm_b_s[j])
            x2, hp, logits_t = _proj_resid_ln(gate_in, gm_w_out[j].astype(BF16), gm_b_out[j], x2, mods, 2 * i,
                                              ln_g[i, 0], ln_b[i, 0], wrt, seq)
        else:
            qkv = _mm(h, at_w_qkv, j, at_b_qkv[j], act=None, tm=ROW_TM, tn=QKV_TN, out_dtype=BF16)
            o = _attention(qkv, at_sinks[j], seq)
            x2, hp, logits_t = _proj_resid_ln(o, at_w_o[j].astype(BF16), at_b_o[j], x2, mods, 2 * i,
                                              ln_g[i, 0], ln_b[i, 0], wrt, seq)
        p = dict(router_bias=moe_router_bias[i], w_gate=moe_w_gate, w_up=moe_w_up, w_down=moe_w_down,
                 ws_gate=moe_ws_gate[i], ws_up=moe_ws_up[i], ws_down=moe_ws_down[i],
                 ln_g=ln_g[i, 1], ln_b=ln_b[i, 1])
        x2, h = _moe(hp, logits_t, x2, mods, 2 * i + 1, p, i, seq, last=(i == depth - 1))
    return x2.reshape(bsz, seq, d)
```

```python
import functools
import math

import numpy as np
import jax
import jax.numpy as jnp
from jax import lax
from jax.experimental import pallas as pl
from jax.experimental.pallas import tpu as pltpu

F32 = jnp.float32
BF16 = jnp.bfloat16

DEPTH = 4
CHUNK = 128
GMLP_GROUPS = 8
HEAD_DIM = 64
Q_PER_KV = 8
ATT_BLOCK = 128
N_EXPERTS = 64
TOP_K = 8
N_EXPERT_GROUPS = 8
TOPK_GROUPS = 4
ROUTED_SCALE = 2.5
DEEPNORM_ALPHA = (2.0 * DEPTH) ** 0.25
LN_EPS = 1e-5
NEG_INF = -1e30

LANES = 128
VMEM_LIMIT = 56 * 1024 * 1024

ROW_TM = 512
GELU_TN = 1024
QKV_TN = 1280
PROJ_TK = 2048
EPILOGUE_SLAB = 128
ADAMOD_TN = 768
ROUTE_TT = 1024
MOE_BM = 512
COMB_TT = 128
DISP_TD = 256


def _cp(sem, vmem=VMEM_LIMIT):
    return pltpu.CompilerParams(dimension_semantics=sem, vmem_limit_bytes=vmem)


def _layer_norm(x, g, b):
    mu = jnp.mean(x, axis=-1, keepdims=True)
    xc = x - mu
    var = jnp.mean(xc * xc, axis=-1, keepdims=True)
    return xc * lax.rsqrt(var + LN_EPS) * g + b


def _alibi_slopes(n):
    p = 2 ** int(math.floor(math.log2(n)))
    base = [2.0 ** (-8.0 * (i + 1) / p) for i in range(p)]
    extra = [2.0 ** (-4.0 * (2 * i + 1) / p) for i in range(n - p)]
    return [float(np.float32(s)) for s in base + extra]


def _adamod_kernel(c_ref, w_ref, b_ref, o_ref):
    c = c_ref[...]
    ca = (c * jax.nn.sigmoid(c)).astype(BF16)
    o_ref[...] = jnp.dot(ca, w_ref[...].astype(BF16), preferred_element_type=F32) + b_ref[...]


def _adamod(c_pad, ada_w, ada_b):
    ns, d, d3 = ada_w.shape
    rows = c_pad.shape[0]
    tn = ADAMOD_TN
    return pl.pallas_call(
        _adamod_kernel,
        grid=(ns, d3 // tn),
        in_specs=[
            pl.BlockSpec((rows, d), lambda s, j: (0, 0)),
            pl.BlockSpec((None, d, tn), lambda s, j: (s, 0, j)),
            pl.BlockSpec((None, 1, tn), lambda s, j: (s, 0, j)),
        ],
        out_specs=pl.BlockSpec((None, rows, tn), lambda s, j: (s, 0, j)),
        out_shape=jax.ShapeDtypeStruct((ns, rows, d3), F32),
        compiler_params=_cp(("arbitrary", "arbitrary")),
        name="adamod",
    )(c_pad, ada_w, ada_b.reshape(ns, 1, d3))


def _modulate_kernel(x_ref, mod_ref, h_ref):
    h = x_ref[...] * (1.0 + mod_ref[1:2, :]) + mod_ref[0:1, :]
    h_ref[...] = h.astype(h_ref.dtype)


def _modulate(x2, mods, sub, seq):
    t, d = x2.shape
    tm = ROW_TM
    per_b = seq // tm
    return pl.pallas_call(
        _modulate_kernel,
        grid=(t // tm,),
        in_specs=[
            pl.BlockSpec((tm, d), lambda i: (i, 0)),
            pl.BlockSpec((None, None, 3, d), lambda i: (sub, i // per_b, 0, 0)),
        ],
        out_specs=pl.BlockSpec((tm, d), lambda i: (i, 0)),
        out_shape=jax.ShapeDtypeStruct((t, d), BF16),
        compiler_params=_cp(("arbitrary",)),
        name="modulate",
    )(x2, mods)


def _mm_kernel(a_ref, w_ref, b_ref, o_ref, wb_ref, *, act):
    @pl.when(pl.program_id(1) == 0)
    def _cast_weights():
        wb_ref[...] = w_ref[...].astype(BF16)

    acc = jnp.dot(a_ref[...], wb_ref[...], preferred_element_type=F32) + b_ref[...]
    if act == "gelu":
        acc = 0.5 * acc * (1.0 + lax.erf(acc * np.float32(math.sqrt(0.5))))
    o_ref[...] = acc.astype(o_ref.dtype)


def _mm(a, w_stack, layer, b, *, act, tm, tn, out_dtype):
    m, k = a.shape
    n = w_stack.shape[2]
    return pl.pallas_call(
        functools.partial(_mm_kernel, act=act),
        grid=(n // tn, m // tm),
        in_specs=[
            pl.BlockSpec((tm, k), lambda j, i: (i, 0)),
            pl.BlockSpec((None, k, tn), lambda j, i: (layer, 0, j)),
            pl.BlockSpec((1, tn), lambda j, i: (0, j)),
        ],
        out_specs=pl.BlockSpec((tm, tn), lambda j, i: (i, j)),
        out_shape=jax.ShapeDtypeStruct((m, n), out_dtype),
        scratch_shapes=[pltpu.VMEM((k, tn), BF16)],
        compiler_params=_cp(("arbitrary", "arbitrary")),
        name="mm_" + (act or "bias"),
    )(a, w_stack, b.reshape(1, n))


def _attn_kernel(sink_ref, q_ref, kc_ref, vc_ref, kp_ref, vp_ref, o_ref, bias_ref, p_ref,
                 *, nb, n_kv, slopes):
    blk = ATT_BLOCK
    pairs = Q_PER_KV // 2
    i = pl.program_id(0)

    @pl.when(i == 0)
    def _build_bias():
        qi = lax.broadcasted_iota(jnp.int32, (blk, 2 * blk), 0)
        kj = lax.broadcasted_iota(jnp.int32, (blk, 2 * blk), 1)
        dist = qi + blk - kj
        valid = (dist >= 0) & (dist < blk)
        distf = dist.astype(F32)
        for g in range(n_kv):
            for p in range(pairs):
                for j in range(2):
                    slope = slopes[g * Q_PER_KV + 2 * p + j]
                    base = jnp.where(valid, -slope * distf, NEG_INF)
                    rs = slice(p * blk, (p + 1) * blk)
                    cs = slice(j * 2 * blk, (j + 1) * 2 * blk)
                    bias_ref[0, g, rs, cs] = base
                    bias_ref[1, g, rs, cs] = jnp.where(kj < blk, NEG_INF, base)

    first = (i % nb == 0).astype(jnp.int32)
    lane = lax.broadcasted_iota(jnp.int32, (2 * blk, LANES), 1)
    lo = lane < HEAD_DIM
    lo_q = lax.broadcasted_iota(jnp.int32, (blk, LANES), 1) < HEAD_DIM

    def blockdiag(prev_ref, cur_ref, g, scale):
        c, half = g // 2, g % 2
        cs = slice(c * LANES, (c + 1) * LANES)
        t2 = jnp.concatenate([prev_ref[:, cs], cur_ref[:, cs]], axis=0).astype(F32) * scale
        rolled = pltpu.roll(t2, HEAD_DIM, 1)
        if half == 0:
            a = jnp.where(lo, t2, 0.0)
            b = jnp.where(lo, 0.0, rolled)
        else:
            a = jnp.where(lo, rolled, 0.0)
            b = jnp.where(lo, 0.0, t2)
        return jnp.concatenate([a, b], axis=0).astype(BF16)

    for g in range(n_kv):
        kbd = blockdiag(kp_ref, kc_ref, g, HEAD_DIM ** -0.5)
        vbd = blockdiag(vp_ref, vc_ref, g, 1.0)
        qg = jnp.concatenate(
            [q_ref[:, (g * pairs + p) * LANES:(g * pairs + p + 1) * LANES] for p in range(pairs)], axis=0)
        s = lax.dot_general(qg, kbd, (((1,), (1,)), ((), ())), preferred_element_type=F32)
        s = s + bias_ref[first, g]
        inv = []
        for p in range(pairs):
            for j in range(2):
                rs = slice(p * blk, (p + 1) * blk)
                cs = slice(j * 2 * blk, (j + 1) * 2 * blk)
                sub = s[rs, cs]
                sink = sink_ref[g * Q_PER_KV + 2 * p + j]
                m = jnp.maximum(jnp.max(sub, axis=-1, keepdims=True), sink)
                e = jnp.exp(sub - m)
                denom = jnp.sum(e, axis=-1, keepdims=True) + jnp.exp(sink - m)
                p_ref[rs, cs] = e.astype(BF16)
                inv.append(1.0 / denom)
        o = jnp.dot(p_ref[...], vbd, preferred_element_type=F32)
        for p in range(pairs):
            scale = jnp.where(lo_q, inv[2 * p], inv[2 * p + 1])
            col = (g * pairs + p) * LANES
            o_ref[:, col:col + LANES] = (o[p * blk:(p + 1) * blk] * scale).astype(o_ref.dtype)


def _attention(qkv, sinks, seq):
    t, w = qkv.shape
    blk = ATT_BLOCK
    nb = seq // blk
    n_kv = (w // HEAD_DIM) // (Q_PER_KV + 2)
    dq = n_kv * Q_PER_KV * HEAD_DIM
    dkv = n_kv * HEAD_DIM
    kcol = dq // dkv
    pairs = Q_PER_KV // 2
    slopes = _alibi_slopes(n_kv * Q_PER_KV)

    def prev(i):
        return jnp.where(i % nb == 0, i, i - 1)

    return pl.pallas_call(
        functools.partial(_attn_kernel, nb=nb, n_kv=n_kv, slopes=slopes),
        grid=(t // blk,),
        in_specs=[
            pl.BlockSpec(memory_space=pltpu.SMEM),
            pl.BlockSpec((blk, dq), lambda i: (i, 0)),
            pl.BlockSpec((blk, dkv), lambda i: (i, kcol)),
            pl.BlockSpec((blk, dkv), lambda i: (i, kcol + 1)),
            pl.BlockSpec((blk, dkv), lambda i: (prev(i), kcol)),
            pl.BlockSpec((blk, dkv), lambda i: (prev(i), kcol + 1)),
        ],
        out_specs=pl.BlockSpec((blk, dq), lambda i: (i, 0)),
        out_shape=jax.ShapeDtypeStruct((t, dq), BF16),
        scratch_shapes=[
            pltpu.VMEM((2, n_kv, pairs * blk, 4 * blk), F32),
            pltpu.VMEM((pairs * blk, 4 * blk), BF16),
        ],
        compiler_params=_cp(("arbitrary",)),
        name="swa_attention",
    )(sinks, qkv, qkv, qkv, qkv, qkv)


SUBLANES = 8


def _rows_from_tiles(x3):
    xt = jnp.transpose(x3, (1, 0, 2))
    return jnp.concatenate([xt[s] for s in range(x3.shape[1])], axis=1)


def _tiles_from_rows(x2):
    xt = jnp.stack([x2[:, s * LANES:(s + 1) * LANES] for s in range(x2.shape[1] // LANES)], axis=0)
    return jnp.transpose(xt, (1, 0, 2))


def _pack_rows(h):
    half = h.shape[1] // 2
    bits = lax.bitcast_convert_type(h.astype(BF16).astype(F32), jnp.uint32)
    return (bits[:, :half] >> 16) | (bits[:, half:] & jnp.uint32(0xFFFF0000))


def _unpack_f32(p):
    lo = lax.bitcast_convert_type(p << 16, F32)
    hi = lax.bitcast_convert_type(p & jnp.uint32(0xFFFF0000), F32)
    return lo, hi


def _unpack_rows(p):
    lo, hi = _unpack_f32(p)
    return lo.astype(BF16), hi.astype(BF16)


def _swiglu_packed(p, wg, wu, wd):
    lo, hi = _unpack_rows(p)
    half = p.shape[1]
    gt = (jnp.dot(lo, wg[:half], preferred_element_type=F32)
          + jnp.dot(hi, wg[half:], preferred_element_type=F32))
    up = (jnp.dot(lo, wu[:half], preferred_element_type=F32)
          + jnp.dot(hi, wu[half:], preferred_element_type=F32))
    act = (gt * jax.nn.sigmoid(gt) * up).astype(BF16)
    return jnp.dot(act, wd, preferred_element_type=F32)


def _resid_ln(y, x, mod_ref, lng_ref, lnb_ref):
    r = DEEPNORM_ALPHA * x + (1.0 + mod_ref[2:3, :]) * y
    return _layer_norm(r, lng_ref[...], lnb_ref[...])


def _proj_kernel(a_ref, w_ref, *rest, nk, slab):
    k = pl.program_id(1)
    acc_ref = rest[-1]

    @pl.when(k == 0)
    def _zero():
        acc_ref[...] = jnp.zeros_like(acc_ref)

    acc_ref[...] += jnp.dot(a_ref[...], w_ref[...], preferred_element_type=F32)
    _proj_finish(k == nk - 1, *rest, slab=slab)


def _gate_proj_kernel(u_ref, v_ref, vall_ref, vg_ref, vb_ref, ws_ref, bs_ref, w_ref, *rest, nk, slab):
    k = pl.program_id(1)
    acc_ref, mu_ref, rstd_ref = rest[-3:]
    tm, gw = u_ref.shape

    @pl.when(k == 0)
    def _start():
        acc_ref[...] = jnp.zeros_like(acc_ref)
        for r0 in range(0, tm, slab):
            rs = slice(r0, r0 + slab)
            v = vall_ref[rs, :].astype(F32)
            mu = jnp.mean(v, axis=-1, keepdims=True)
            var = jnp.mean(v * v, axis=-1, keepdims=True) - mu * mu
            mu_ref[rs, :] = jnp.broadcast_to(mu, (slab, LANES))
            rstd_ref[rs, :] = jnp.broadcast_to(lax.rsqrt(var + LN_EPS), (slab, LANES))

    row = lax.broadcasted_iota(jnp.int32, (CHUNK, CHUNK), 0)
    col = lax.broadcasted_iota(jnp.int32, (CHUNK, CHUNK), 1)
    wsg = jnp.where(row >= col, ws_ref[k], 0.0).astype(BF16)
    bs = bs_ref[k][:, 0:1]
    vn = ((v_ref[...].astype(F32) - mu_ref[:, 0:1]) * rstd_ref[:, 0:1] * vg_ref[...]
          + vb_ref[...]).astype(BF16)
    parts = []
    for c in range(tm // CHUNK):
        rs = slice(c * CHUNK, (c + 1) * CHUNK)
        s = jnp.dot(wsg, vn[rs], preferred_element_type=F32) + bs
        parts.append((u_ref[rs, :].astype(F32) * s).astype(BF16))
    a = jnp.concatenate(parts, axis=0)
    acc_ref[...] += jnp.dot(a, w_ref[...], preferred_element_type=F32)
    _proj_finish(k == nk - 1, *rest[:-2], slab=slab)


def _proj_finish(is_last, b_ref, x_ref, mod_ref, nmod_ref, lng_ref, lnb_ref, wrh_ref, wrl_ref,
                 xo_ref, hp_ref, lt_ref, acc_ref, *, slab):
    @pl.when(is_last)
    def _finish():
        for r0 in range(0, acc_ref.shape[0], slab):
            rs = slice(r0, r0 + slab)
            xn = _resid_ln(acc_ref[rs, :] + b_ref[...], x_ref[rs, :], mod_ref, lng_ref, lnb_ref)
            xo_ref[rs, :] = xn
            h = xn * (1.0 + nmod_ref[1:2, :]) + nmod_ref[0:1, :]
            hp_ref[rs] = _tiles_from_rows(_pack_rows(h))
            h_hi = h.astype(BF16)
            h_lo = (h - h_hi.astype(F32)).astype(BF16)
            lg = (jnp.dot(h_hi, wrh_ref[...], preferred_element_type=F32)
                  + (jnp.dot(h_hi, wrl_ref[...], preferred_element_type=F32)
                     + jnp.dot(h_lo, wrh_ref[...], preferred_element_type=F32)))
            lt_ref[:, rs] = jnp.transpose(lg)[:lt_ref.shape[0]]


def _proj_resid_ln(lhs, w, b, x2, mods, sub, ln_g, ln_b, w_router, seq):
    t, d = x2.shape
    tm = ROW_TM
    per_b = seq // tm
    ne = w_router.shape[1]
    wr = jnp.zeros((d, LANES), F32).at[:, :ne].set(w_router)
    wr_hi = wr.astype(BF16)
    wr_lo = (wr - wr_hi.astype(F32)).astype(BF16)
    row = lambda i, k: (i, 0)
    const = lambda i, k: (0, 0)
    const3 = lambda i, k: (0, 0, 0)
    if isinstance(lhs, tuple):
        z, vg, vb, w_s, b_s = lhs
        half = z.shape[1] // 2
        nk = GMLP_GROUPS
        tk = half // nk
        body = _gate_proj_kernel
        lhs_args = (z, z, z, vg.reshape(1, half), vb.reshape(1, half), w_s,
                    jnp.broadcast_to(b_s[:, :, None], b_s.shape + (LANES,)))
        lhs_specs = [
            pl.BlockSpec((tm, tk), lambda i, k: (i, k)),
            pl.BlockSpec((tm, tk), lambda i, k: (i, nk + k)),
            pl.BlockSpec((tm, half), lambda i, k: (i, 1)),
            pl.BlockSpec((1, tk), lambda i, k: (0, k)),
            pl.BlockSpec((1, tk), lambda i, k: (0, k)),
            pl.BlockSpec(w_s.shape, const3),
            pl.BlockSpec(b_s.shape + (LANES,), const3),
        ]
        extra_scratch = [pltpu.VMEM((tm, LANES), F32), pltpu.VMEM((tm, LANES), F32)]
    else:
        kdim = lhs.shape[1]
        tk = min(kdim, PROJ_TK)
        nk = kdim // tk
        body = _proj_kernel
        lhs_args = (lhs,)
        lhs_specs = [pl.BlockSpec((tm, tk), lambda i, k: (i, k))]
        extra_scratch = []
    return pl.pallas_call(
        functools.partial(body, nk=nk, slab=EPILOGUE_SLAB),
        grid=(t // tm, nk),
        in_specs=lhs_specs + [
            pl.BlockSpec((tk, d), lambda i, k: (k, 0)),
            pl.BlockSpec((1, d), const),
            pl.BlockSpec((tm, d), row),
            pl.BlockSpec((None, None, 3, d), lambda i, k: (sub, i // per_b, 0, 0)),
            pl.BlockSpec((None, None, 3, d), lambda i, k: (sub + 1, i // per_b, 0, 0)),
            pl.BlockSpec((1, d), const),
            pl.BlockSpec((1, d), const),
            pl.BlockSpec((d, LANES), const),
            pl.BlockSpec((d, LANES), const),
        ],
        out_specs=[
            pl.BlockSpec((tm, d), row),
            pl.BlockSpec((tm, SUBLANES, LANES), lambda i, k: (i, 0, 0)),
            pl.BlockSpec((ne, tm), lambda i, k: (0, i)),
        ],
        out_shape=[
            jax.ShapeDtypeStruct((t, d), F32),
            jax.ShapeDtypeStruct((t, SUBLANES, LANES), jnp.uint32),
            jax.ShapeDtypeStruct((ne, t), F32),
        ],
        scratch_shapes=[pltpu.VMEM((tm, d), F32)] + extra_scratch,
        compiler_params=_cp(("arbitrary", "arbitrary")),
        name="proj_resid_ln",
    )(*lhs_args, w, b.reshape(1, d), x2, mods, mods, ln_g.reshape(1, d), ln_b.reshape(1, d), wr_hi, wr_lo)


def _route_kernel(lg_ref, bias_ref, eidx_ref, rank_ref, w_ref, cnt_ref, tri_ref, carry_ref):
    i = pl.program_id(0)
    ne, tt = lg_ref.shape
    gsz = ne // N_EXPERT_GROUPS
    neg = -jnp.inf

    @pl.when(i == 0)
    def _init():
        r = lax.broadcasted_iota(jnp.int32, (tt, tt), 0)
        c = lax.broadcasted_iota(jnp.int32, (tt, tt), 1)
        tri_ref[...] = jnp.where(r < c, 1.0, 0.0).astype(BF16)
        carry_ref[...] = jnp.zeros_like(carry_ref)

    def amax(v, ids, limit):
        m = jnp.max(v, axis=0, keepdims=True)
        idx = jnp.min(jnp.where(v == m, ids, limit), axis=0, keepdims=True)
        return m, idx

    s = jax.nn.sigmoid(lg_ref[...])
    sel = s + bias_ref[:, 0:1]

    ids_g = lax.broadcasted_iota(jnp.int32, (gsz, tt), 0).astype(F32)
    gscore = []
    for g in range(N_EXPERT_GROUPS):
        xg = sel[g * gsz:(g + 1) * gsz]
        m1, i1 = amax(xg, ids_g, float(gsz))
        m2 = jnp.max(jnp.where(ids_g == i1, neg, xg), axis=0, keepdims=True)
        gscore.append(m1 + m2)
    work = jnp.concatenate(gscore, axis=0)
    ids_n = lax.broadcasted_iota(jnp.int32, (N_EXPERT_GROUPS, tt), 0).astype(F32)
    gsel = jnp.zeros((N_EXPERT_GROUPS, tt), F32)
    for _ in range(TOPK_GROUPS):
        _, ig = amax(work, ids_n, float(N_EXPERT_GROUPS))
        pick = ids_n == ig
        gsel = jnp.where(pick, 1.0, gsel)
        work = jnp.where(pick, neg, work)
    work = jnp.concatenate(
        [jnp.where(gsel[g:g + 1] > 0.5, sel[g * gsz:(g + 1) * gsz], NEG_INF)
         for g in range(N_EXPERT_GROUPS)], axis=0)

    ids_e = lax.broadcasted_iota(jnp.int32, (ne, tt), 0).astype(F32)
    chosen = jnp.zeros((ne, tt), F32)
    picked, svals = [], []
    for _ in range(TOP_K):
        _, ik = amax(work, ids_e, float(ne))
        pick = ids_e == ik
        chosen = jnp.where(pick, 1.0, chosen)
        work = jnp.where(pick, neg, work)
        picked.append(ik)
        svals.append(jnp.sum(jnp.where(pick, s, 0.0), axis=0, keepdims=True))
    denom = svals[0]
    for k in range(1, TOP_K):
        denom = denom + svals[k]

    prefix = jnp.dot(chosen.astype(BF16), tri_ref[...], preferred_element_type=F32)
    rank_dense = prefix + carry_ref[:, 0:1]
    carry_ref[...] = carry_ref[...] + jnp.sum(chosen, axis=1, keepdims=True)
    cnt_ref[...] = carry_ref[...]
    for k in range(TOP_K):
        pick = ids_e == picked[k]
        eidx_ref[k:k + 1, :] = picked[k].astype(jnp.int32)
        rank_ref[k:k + 1, :] = jnp.sum(jnp.where(pick, rank_dense, 0.0), axis=0,
                                       keepdims=True).astype(jnp.int32)
        w_ref[k:k + 1, :] = svals[k] / denom * ROUTED_SCALE


def _route(logits_t, router_bias):
    ne, t = logits_t.shape
    tt = ROUTE_TT
    bias = jnp.broadcast_to(router_bias.astype(F32)[:, None], (ne, LANES))
    tok = lambda i: (0, i)
    const = lambda i: (0, 0)
    return pl.pallas_call(
        _route_kernel,
        grid=(t // tt,),
        in_specs=[pl.BlockSpec((ne, tt), tok), pl.BlockSpec((ne, LANES), const)],
        out_specs=[
            pl.BlockSpec((TOP_K, tt), tok),
            pl.BlockSpec((TOP_K, tt), tok),
            pl.BlockSpec((TOP_K, tt), tok),
            pl.BlockSpec((ne, LANES), const),
        ],
        out_shape=[
            jax.ShapeDtypeStruct((TOP_K, t), jnp.int32),
            jax.ShapeDtypeStruct((TOP_K, t), jnp.int32),
            jax.ShapeDtypeStruct((TOP_K, t), F32),
            jax.ShapeDtypeStruct((ne, LANES), F32),
        ],
        scratch_shapes=[pltpu.VMEM((tt, tt), BF16), pltpu.VMEM((ne, LANES), F32)],
        compiler_params=_cp(("arbitrary",)),
        name="moe_route",
    )(logits_t, bias)


def _dest_kernel(pstart_ref, eidx_ref, rank_ref, dest_ref):
    e = eidx_ref[...]
    base = jnp.zeros_like(e)
    for x in range(N_EXPERTS):
        base = jnp.where(e == x, pstart_ref[x], base)
    dest_ref[...] = rank_ref[...] + base


def _dest_rows(pstart, eidx, rank):
    return pl.pallas_call(
        _dest_kernel,
        in_specs=[pl.BlockSpec(memory_space=pltpu.SMEM),
                  pl.BlockSpec(memory_space=pltpu.VMEM),
                  pl.BlockSpec(memory_space=pltpu.VMEM)],
        out_specs=pl.BlockSpec(memory_space=pltpu.VMEM),
        out_shape=jax.ShapeDtypeStruct(eidx.shape, jnp.int32),
        name="moe_dest",
    )(pstart, eidx, rank)


def _dispatch_kernel(pend_ref, has_ref, dest_ref, hp_ref, wg_ref, wu_ref, wd_ref, xs_hbm, sh_ref,
                     zbuf, sem, zsem):
    i = pl.program_id(0)
    td = hp_ref.shape[0]
    bm = zbuf.shape[0]

    @pl.when(i == 0)
    def _zero_tails():
        zbuf[...] = jnp.zeros_like(zbuf)

        def tail(e):
            start_row = pl.multiple_of(pend_ref[e] - bm, bm)
            return pltpu.make_async_copy(zbuf, xs_hbm.at[pl.ds(start_row, bm)], zsem)

        def start(e, carry):
            @pl.when(has_ref[e] == 1)
            def _():
                tail(e).start()
            return carry

        def wait(e, carry):
            @pl.when(has_ref[e] == 1)
            def _():
                tail(e).wait()
            return carry

        lax.fori_loop(0, N_EXPERTS, start, 0)
        lax.fori_loop(0, N_EXPERTS, wait, 0)

    def row(tok, k):
        return pltpu.make_async_copy(hp_ref.at[pl.ds(tok, 1)],
                                     xs_hbm.at[pl.ds(dest_ref[k * td + tok], 1)], sem)

    def start_rows(g, carry):
        for j in range(SUBLANES):
            for k in range(TOP_K):
                row(g * SUBLANES + j, k).start(priority=k % 2)
        return carry

    def wait_rows(g, carry):
        for j in range(SUBLANES):
            for k in range(TOP_K):
                row(g * SUBLANES + j, k).wait()
        return carry

    lax.fori_loop(0, td // SUBLANES, start_rows, 0)
    sh_ref[...] = _swiglu_packed(_rows_from_tiles(hp_ref[...]), wg_ref[...], wu_ref[...], wd_ref[...])
    lax.fori_loop(0, td // SUBLANES, wait_rows, 0)


def _dispatch(hp, dest_tiles, pends, has, n_rows, td, wg, wu, wd):
    t = hp.shape[0]
    tile = hp.shape[1:]
    d, ff = wg.shape
    const = lambda i, pe, ha: (0, 0)
    grid_spec = pltpu.PrefetchScalarGridSpec(
        num_scalar_prefetch=2,
        grid=(t // td,),
        in_specs=[
            pl.BlockSpec((TOP_K * td,), lambda i, pe, ha: (i,), memory_space=pltpu.SMEM),
            pl.BlockSpec((td,) + tile, lambda i, pe, ha: (i, 0, 0)),
            pl.BlockSpec((d, ff), const),
            pl.BlockSpec((d, ff), const),
            pl.BlockSpec((ff, d), const),
        ],
        out_specs=[pl.BlockSpec(memory_space=pl.ANY),
                   pl.BlockSpec((td, d), lambda i, pe, ha: (i, 0))],
        scratch_shapes=[pltpu.VMEM((MOE_BM,) + tile, jnp.uint32), pltpu.SemaphoreType.DMA(()),
                        pltpu.SemaphoreType.DMA(())],
    )
    return pl.pallas_call(
        _dispatch_kernel,
        grid_spec=grid_spec,
        out_shape=[jax.ShapeDtypeStruct((n_rows,) + tile, jnp.uint32),
                   jax.ShapeDtypeStruct((t, d), F32)],
        compiler_params=_cp(("arbitrary",)),
        name="moe_dispatch",
    )(pends, has, dest_tiles, hp, wg, wu, wd)


def _experts_kernel(blk_e_ref, blk_v_ref, blk_src_ref, xs_ref, wg_ref, wu_ref, wd_ref, o_ref,
                    wgb, wub, wdb):
    del blk_src_ref
    b = pl.program_id(0)
    valid = blk_v_ref[b] == 1
    changed = (b == 0) | (blk_e_ref[b] != blk_e_ref[jnp.maximum(b - 1, 0)])

    @pl.when(valid & changed)
    def _cast_weights():
        wgb[...] = wg_ref[...].astype(BF16)
        wub[...] = wu_ref[...].astype(BF16)
        wdb[...] = wd_ref[...].astype(BF16)

    @pl.when(valid)
    def _compute():
        y = _swiglu_packed(_rows_from_tiles(xs_ref[...]), wgb[...], wub[...], wdb[...])
        o_ref[...] = _tiles_from_rows(_pack_rows(y))


def _experts(xs, blk_e, blk_v, blk_src, wg, wu, wd, layer):
    _, ne, d, ff = wg.shape
    bm = MOE_BM
    nblk = blk_e.shape[0]
    tile = xs.shape[1:]
    wmap = lambda b, be, bv, bs: (layer, be[b], 0, 0)
    rows = lambda b, be, bv, bs: (bs[b], 0, 0)
    grid_spec = pltpu.PrefetchScalarGridSpec(
        num_scalar_prefetch=3,
        grid=(nblk,),
        in_specs=[
            pl.BlockSpec((bm,) + tile, rows),
            pl.BlockSpec((None, None, d, ff), wmap),
            pl.BlockSpec((None, None, d, ff), wmap),
            pl.BlockSpec((None, None, ff, d), wmap),
        ],
        out_specs=pl.BlockSpec((bm,) + tile, rows),
        scratch_shapes=[pltpu.VMEM((d, ff), BF16), pltpu.VMEM((d, ff), BF16), pltpu.VMEM((ff, d), BF16)],
    )
    return pl.pallas_call(
        _experts_kernel,
        grid_spec=grid_spec,
        out_shape=jax.ShapeDtypeStruct((nblk * bm,) + tile, jnp.uint32),
        compiler_params=_cp(("arbitrary",)),
        name="moe_experts",
    )(blk_e, blk_v, blk_src, xs, wg, wu, wd)


def _combine_kernel(dcur_ref, dnxt_ref, y_hbm, wts_ref, sh_ref, x_ref, mod_ref, nmod_ref, lng_ref,
                    lnb_ref, *rest, last):
    if last:
        xo_ref, ybuf, sem = rest
    else:
        xo_ref, ho_ref, ybuf, sem = rest
    tt = x_ref.shape[0]
    i = pl.program_id(0)
    slot = i % 2

    def row(dref, slot_, tok, k):
        return pltpu.make_async_copy(y_hbm.at[pl.ds(dref[k * tt + tok], 1)],
                                     ybuf.at[slot_, pl.ds(k * tt + tok, 1)], sem.at[slot_])

    grp = 2 * SUBLANES
    ngrp = tt // grp

    def tile_loop(dref, slot_, op):
        def body(g, carry):
            for j in range(grp):
                for k in range(TOP_K):
                    copy = row(dref, slot_, g * grp + j, k)
                    if op == "start":
                        copy.start(priority=k % 2)
                    else:
                        copy.wait()
            return carry

        lax.fori_loop(0, ngrp, body, 0)

    @pl.when(i == 0)
    def _prime():
        tile_loop(dcur_ref, 0, "start")

    @pl.when(i + 1 < pl.num_programs(0))
    def _prefetch():
        tile_loop(dnxt_ref, 1 - slot, "start")

    tile_loop(dcur_ref, slot, "wait")

    wb = [jnp.broadcast_to(wts_ref[:, k:k + 1], (tt, LANES)) for k in range(TOP_K)]
    lo_cols = [None] * SUBLANES
    hi_cols = [None] * SUBLANES
    for k in range(TOP_K):
        yt = jnp.transpose(ybuf[slot, pl.ds(k * tt, tt)], (1, 0, 2))
        for s in range(SUBLANES):
            lo, hi = _unpack_f32(yt[s])
            lo_cols[s] = lo * wb[k] if k == 0 else lo_cols[s] + lo * wb[k]
            hi_cols[s] = hi * wb[k] if k == 0 else hi_cols[s] + hi * wb[k]
    routed = jnp.concatenate(lo_cols + hi_cols, axis=1)
    xn = _resid_ln(routed + sh_ref[...], x_ref[...], mod_ref, lng_ref, lnb_ref)
    xo_ref[...] = xn
    if not last:
        ho_ref[...] = (xn * (1.0 + nmod_ref[1:2, :]) + nmod_ref[0:1, :]).astype(ho_ref.dtype)


def _combine(ys, dest_tiles, wts, shared, x2, mods, sub, ln_g, ln_b, seq, last):
    t, d = x2.shape
    tt = COMB_TT
    nt = t // tt
    per_b = seq // tt
    nsub = sub if last else sub + 1
    row = lambda i: (i, 0)
    const = lambda i: (0, 0)
    outs = pl.pallas_call(
        functools.partial(_combine_kernel, last=last),
        grid=(nt,),
        in_specs=[
            pl.BlockSpec((tt * TOP_K,), lambda i: (i,), memory_space=pltpu.SMEM),
            pl.BlockSpec((tt * TOP_K,), lambda i: (jnp.minimum(i + 1, nt - 1),), memory_space=pltpu.SMEM),
            pl.BlockSpec(memory_space=pl.ANY),
            pl.BlockSpec((tt, TOP_K), row),
            pl.BlockSpec((tt, d), row),
            pl.BlockSpec((tt, d), row),
            pl.BlockSpec((None, None, 3, d), lambda i: (sub, i // per_b, 0, 0)),
            pl.BlockSpec((None, None, 3, d), lambda i: (nsub, i // per_b, 0, 0)),
            pl.BlockSpec((1, d), const),
            pl.BlockSpec((1, d), const),
        ],
        out_specs=[pl.BlockSpec((tt, d), row)] * (1 if last else 2),
        out_shape=[jax.ShapeDtypeStruct((t, d), F32)] + ([] if last else [jax.ShapeDtypeStruct((t, d), BF16)]),
        scratch_shapes=[pltpu.VMEM((2, TOP_K * tt) + ys.shape[1:], jnp.uint32),
                        pltpu.SemaphoreType.DMA((2,))],
        compiler_params=_cp(("arbitrary",)),
        name="moe_combine",
    )(dest_tiles, dest_tiles, ys, wts, shared, x2, mods, mods, ln_g.reshape(1, d), ln_b.reshape(1, d))
    return (outs[0], None) if last else (outs[0], outs[1])


def _tile_major(a, tile):
    k, t = a.shape
    return a.reshape(k, t // tile, tile).transpose(1, 0, 2).reshape(-1)


def _moe(hp, logits_t, x2, mods, sub, p, layer, seq, last):
    t = hp.shape[0]
    bm = MOE_BM
    eidx, rank, w_t, cnt = _route(logits_t, p["router_bias"])
    counts = cnt[:, 0].astype(jnp.int32)
    padded = (counts + bm - 1) // bm * bm
    pends = jnp.cumsum(padded)
    pstart = pends - padded
    nblk = (t * TOP_K) // bm + N_EXPERTS
    blk_ids = jnp.arange(nblk, dtype=jnp.int32)
    blk_v = (blk_ids * bm < pends[-1]).astype(jnp.int32)
    blk_src = jnp.minimum(blk_ids, pends[-1] // bm - 1).astype(jnp.int32)
    blk_e = jnp.sum((blk_src * bm)[:, None] >= pends[None, :], axis=1).astype(jnp.int32)
    has = (counts > 0).astype(jnp.int32)

    dest = _dest_rows(pstart.astype(jnp.int32), eidx, rank)
    xs, sh = _dispatch(hp, _tile_major(dest, DISP_TD), pends.astype(jnp.int32), has, nblk * bm, DISP_TD,
                       p["ws_gate"].astype(BF16), p["ws_up"].astype(BF16), p["ws_down"].astype(BF16))
    ys = _experts(xs, blk_e, blk_v, blk_src, p["w_gate"], p["w_up"], p["w_down"], layer)
    return _combine(ys, _tile_major(dest, COMB_TT), jnp.transpose(w_t), sh, x2, mods, sub,
                    p["ln_g"], p["ln_b"], seq, last)


def kernel(x, c, ada_w, ada_b, ln_g, ln_b, gm_w_in, gm_b_in, gm_v_ln_g, gm_v_ln_b, gm_w_s, gm_b_s,
           gm_w_out, gm_b_out, at_w_qkv, at_b_qkv, at_sinks, at_w_o, at_b_o, moe_w_router,
           moe_router_bias, moe_w_gate, moe_w_up, moe_w_down, moe_ws_gate, moe_ws_up, moe_ws_down):
    bsz, seq, d = x.shape
    t = bsz * seq
    depth = ada_w.shape[0]
    x2 = x.reshape(t, d)

    rows = 8
    c_pad = jnp.zeros((rows, d), F32).at[:bsz].set(c)
    mods = _adamod(c_pad, ada_w.reshape(depth * 2, d, 3 * d), ada_b.reshape(depth * 2, 3 * d))
    mods = mods.reshape(depth * 2, rows, 3, d)

    h = _modulate(x2, mods, 0, seq)
    for i in range(depth):
        j = i // 2
        wrt = moe_w_router[i]
        if i % 2 == 0:
            z = _mm(h, gm_w_in, j, gm_b_in[j], act="gelu", tm=ROW_TM, tn=GELU_TN, out_dtype=BF16)
            gate_in = (z, gm_v_ln_g[j], gm_v_ln_b[j], gm_w_s[j], gm_b_s[j])
            x2, hp, logits_t = _proj_resid_ln(gate_in, gm_w_out[j].astype(BF16), gm_b_out[j], x2, mods, 2 * i,
                                              ln_g[i, 0], ln_b[i, 0], wrt, seq)
        else:
            qkv = _mm(h, at_w_qkv, j, at_b_qkv[j], act=None, tm=ROW_TM, tn=QKV_TN, out_dtype=BF16)
            o = _attention(qkv, at_sinks[j], seq)
            x2, hp, logits_t = _proj_resid_ln(o, at_w_o[j].astype(BF16), at_b_o[j], x2, mods, 2 * i,
                                              ln_g[i, 0], ln_b[i, 0], wrt, seq)
        p = dict(router_bias=moe_router_bias[i], w_gate=moe_w_gate, w_up=moe_w_up, w_down=moe_w_down,
                 ws_gate=moe_ws_gate[i], ws_up=moe_ws_up[i], ws_down=moe_ws_down[i],
                 ln_g=ln_g[i, 1], ln_b=ln_b[i, 1])
        x2, h = _moe(hp, logits_t, x2, mods, 2 * i + 1, p, i, seq, last=(i == depth - 1))
    return x2.reshape(bsz, seq, d)
```
